```python
import math
import jax, jax.numpy as jnp
from jax import lax
import numpy as np

D_MODEL = 1024
BATCH = 8
SEQ = 4096
DEPTH = 1

CHUNK = 64
MIX_A_WIDTH = D_MODEL // 2
MIX_B_WIDTH = D_MODEL - MIX_A_WIDTH
A_GROUPS = 8
A_GROUP_DIM = MIX_A_WIDTH // A_GROUPS
A_BLOCK = 128
B_HEADS = 8
B_HEAD_DIM = MIX_B_WIDTH // B_HEADS
B_LEFT_CHUNKS = 8
B_BAND = (B_LEFT_CHUNKS + 1) * CHUNK
REL_CLIP = 128
PEER_HEADS = 8
PEER_NKEYS = 128
PEER_EXPERTS = PEER_NKEYS * PEER_NKEYS
PEER_QDIM = 256
PEER_HALF = PEER_QDIM // 2
PEER_TOPK = 16
PEER_TOKEN_BLOCK = 128
IN_COLS = 2 * MIX_A_WIDTH + 3 * MIX_B_WIDTH
EPS = 1e-6
NEG_INF = -1e30

kernel_name = "hybrid_gmlp_chunkattn_peer_block"


def rms_norm(x, g):
    xf = x.astype(jnp.float32)
    y = xf * lax.rsqrt(jnp.mean(xf * xf, axis=-1, keepdims=True) + EPS)
    return (y * g.astype(jnp.float32)).astype(x.dtype)


def gmlp_spatial_gating(u, v, norm_g, w_s, b_s):
    bsz, s, _ = u.shape
    u = jax.nn.gelu(u)
    v = rms_norm(jax.nn.gelu(v), norm_g)
    nblk = s // A_BLOCK
    v = v.reshape(bsz, nblk, A_BLOCK, A_GROUPS, A_GROUP_DIM)
    pos = jnp.arange(A_BLOCK)
    mask = (pos[None, :] // CHUNK) <= (pos[:, None] // CHUNK)
    w = jnp.where(mask[None], w_s, jnp.zeros_like(w_s)).astype(v.dtype)
    sp = jnp.einsum('gij,bnjgc->bnigc', w, v) + b_s.T.astype(v.dtype)[None, None, :, :, None]
    return u * sp.reshape(bsz, s, MIX_A_WIDTH)


def banded_chunk_attention(q, k, v, q_g, k_g, rel_bias):
    bsz, s, _ = q.shape
    nc = s // CHUNK
    pad = B_LEFT_CHUNKS * CHUNK
    q = rms_norm(q.reshape(bsz, nc, CHUNK, B_HEADS, B_HEAD_DIM), q_g)
    k = rms_norm(k.reshape(bsz, s, B_HEADS, B_HEAD_DIM), k_g)
    v = v.reshape(bsz, s, B_HEADS, B_HEAD_DIM)
    padw = ((0, 0), (pad, 0), (0, 0), (0, 0))
    kp = jnp.pad(k, padw).reshape(bsz, nc + B_LEFT_CHUNKS, CHUNK, B_HEADS, B_HEAD_DIM)
    vp = jnp.pad(v, padw).reshape(bsz, nc + B_LEFT_CHUNKS, CHUNK, B_HEADS, B_HEAD_DIM)
    band_idx = jnp.arange(nc)[:, None] + jnp.arange(B_LEFT_CHUNKS + 1)[None, :]
    kb = kp[:, band_idx].reshape(bsz, nc, B_BAND, B_HEADS, B_HEAD_DIM)
    vb = vp[:, band_idx].reshape(bsz, nc, B_BAND, B_HEADS, B_HEAD_DIM)
    scores = jnp.einsum('bcihd,bcjhd->bhcij', q, kb).astype(jnp.float32) * (B_HEAD_DIM ** -0.5)
    qi = jnp.arange(CHUNK)[:, None] + pad
    kj = jnp.arange(B_BAND)[None, :]
    rel_idx = jnp.clip(qi - kj, -REL_CLIP, REL_CLIP) + REL_CLIP
    bias = rel_bias.astype(jnp.float32)[:, rel_idx]
    valid = kj >= (pad - jnp.arange(nc)[:, None] * CHUNK)
    scores = scores + bias[None, :, None]
    scores = jnp.where(valid[None, None, :, None, :], scores, NEG_INF)
    p = jax.nn.softmax(scores, axis=-1).astype(v.dtype)
    out = jnp.einsum('bhcij,bcjhd->bcihd', p, vb)
    return out.reshape(bsz, s, MIX_B_WIDTH)


def peer_ffn(x, w_query, sub_keys, expert_u, expert_v):
    bsz, s, d = x.shape
    tokens = x.reshape(-1, PEER_TOKEN_BLOCK, d)

    def block(xb):
        q = (xb @ w_query).reshape(PEER_TOKEN_BLOCK, PEER_HEADS, 2, PEER_HALF)
        sc = jnp.einsum('thpd,hpkd->thpk', q, sub_keys).astype(jnp.float32)
        s_top, i_top = lax.top_k(sc, PEER_TOPK)
        cand = s_top[:, :, 0, :, None] + s_top[:, :, 1, None, :]
        cand_idx = i_top[:, :, 0, :, None] * PEER_NKEYS + i_top[:, :, 1, None, :]
        cand = cand.reshape(PEER_TOKEN_BLOCK, PEER_HEADS, PEER_TOPK * PEER_TOPK)
        cand_idx = cand_idx.reshape(PEER_TOKEN_BLOCK, PEER_HEADS, PEER_TOPK * PEER_TOPK)
        best, pos = lax.top_k(cand, PEER_TOPK)
        eidx = jnp.take_along_axis(cand_idx, pos, axis=-1)
        g = jax.nn.softmax(best, axis=-1)
        u = jnp.take(expert_u, eidx, axis=0)
        h = jax.nn.gelu(jnp.einsum('thkd,td->thk', u, xb))
        vv = jnp.take(expert_v, eidx, axis=0)
        return jnp.einsum('thk,thkd->td', (g * h.astype(jnp.float32)).astype(xb.dtype), vv)

    return lax.map(block, tokens).reshape(bsz, s, d)


def setup_inputs(seed: int = 0) -> dict:
    key = jax.random.key(seed)
    ks = jax.random.split(key, 16)
    f32 = jnp.float32
    nrm = lambda k, shape, sc: jax.random.normal(k, shape, f32) * sc
    return {
        "x": nrm(ks[0], (BATCH, SEQ, D_MODEL), 1.0),
        "ln_mix_g": 1.0 + nrm(ks[1], (DEPTH, D_MODEL), 0.02),
        "w_in": nrm(ks[2], (DEPTH, D_MODEL, IN_COLS), D_MODEL ** -0.5),
        "gmlp_norm_g": 1.0 + nrm(ks[3], (DEPTH, MIX_A_WIDTH), 0.02),
        "gmlp_w_s": nrm(ks[4], (DEPTH, A_GROUPS, A_BLOCK, A_BLOCK), A_BLOCK ** -0.5),
        "gmlp_b_s": 1.0 + nrm(ks[5], (DEPTH, A_GROUPS, A_BLOCK), 0.1),
        "q_norm_g": 1.0 + nrm(ks[6], (DEPTH, B_HEAD_DIM), 0.02),
        "k_norm_g": 1.0 + nrm(ks[7], (DEPTH, B_HEAD_DIM), 0.02),
        "rel_bias": nrm(ks[8], (DEPTH, B_HEADS, 2 * REL_CLIP + 1), 0.1),
        "w_out": nrm(ks[9], (DEPTH, D_MODEL, D_MODEL), D_MODEL ** -0.5),
        "ln_ffn_g": 1.0 + nrm(ks[10], (DEPTH, D_MODEL), 0.02),
        "peer_w_query": nrm(ks[11], (DEPTH, D_MODEL, PEER_HEADS * PEER_QDIM), D_MODEL ** -0.5),
        "peer_sub_keys": nrm(ks[12], (DEPTH, PEER_HEADS, 2, PEER_NKEYS, PEER_HALF), PEER_HALF ** -0.5),
        "peer_u": nrm(ks[13], (DEPTH, PEER_EXPERTS, D_MODEL), D_MODEL ** -0.5),
        "peer_v": nrm(ks[14], (DEPTH, PEER_EXPERTS, D_MODEL), 0.1),
    }


def reference(x, ln_mix_g, w_in, gmlp_norm_g, gmlp_w_s, gmlp_b_s, q_norm_g, k_norm_g, rel_bias,
              w_out, ln_ffn_g, peer_w_query, peer_sub_keys, peer_u, peer_v):
    a0, a1 = MIX_A_WIDTH, 2 * MIX_A_WIDTH
    q0, k0, v0 = a1, a1 + MIX_B_WIDTH, a1 + 2 * MIX_B_WIDTH
    for l in range(DEPTH):
        xn = rms_norm(x, ln_mix_g[l])
        proj = xn @ w_in[l]
        a_out = gmlp_spatial_gating(proj[..., :a0], proj[..., a0:a1],
                                    gmlp_norm_g[l], gmlp_w_s[l], gmlp_b_s[l])
        b_out = banded_chunk_attention(proj[..., q0:k0], proj[..., k0:v0], proj[..., v0:],
                                       q_norm_g[l], k_norm_g[l], rel_bias[l])
        x = x + jnp.concatenate([a_out, b_out], axis=-1) @ w_out[l]
        x = x + peer_ffn(rms_norm(x, ln_ffn_g[l]), peer_w_query[l], peer_sub_keys[l],
                         peer_u[l], peer_v[l])
    return x
```

```python
import functools

import jax
import jax.numpy as jnp
from jax import lax
from jax.experimental import pallas as pl
from jax.experimental.pallas import tpu as pltpu

F32 = jnp.float32
BF16 = jnp.bfloat16

D_MODEL = 1024
CHUNK = 64
MIX_A = 512
MIX_B = 512
A_GROUPS = 8
A_BLOCK = 128
B_HEADS = 8
B_HEAD_DIM = 64
B_LEFT = 8 * CHUNK
REL_CLIP = 128
PEER_HEADS = 8
PEER_NKEYS = 128
PEER_HALF = 128
PEER_TOPK = 16
PEER_PAIRS = PEER_HEADS * PEER_TOPK
EPS = 1e-6
NEG_INF = -1e30

LANES = 128
Q_TILE = 128
K_WIN = B_LEFT + Q_TILE
PROJ_ROWS = 512
TOK_TILE = 128
ROW_WORDS = D_MODEL // 2
ROW_SUB = ROW_WORDS // LANES
PSTRIDE = PEER_PAIRS + 8
VMEM_LIMIT = 48 * 1024 * 1024


def _rms(x, g):
    return x * lax.rsqrt(jnp.mean(x * x, axis=-1, keepdims=True) + EPS) * g


def _split_bf16(x):
    hi = x.astype(BF16)
    lo = (x - hi.astype(F32)).astype(BF16)
    return hi, lo


def _proj_kernel(x_ref, g_ref, w_ref, gn_ref, qg_ref, kg_ref, bd_ref,
                 ug_ref, vn_ref, q_ref, k_ref, v_ref):
    xb = _rms(x_ref[...], g_ref[...]).astype(BF16)

    def seg(j):
        return jnp.dot(xb, w_ref[:, j * MIX_A:(j + 1) * MIX_A], preferred_element_type=F32)

    def head_norm(z, gain):
        hi, lo = _split_bf16(z * z)
        bd = bd_ref[...]
        ss = (jnp.dot(hi, bd, preferred_element_type=F32)
              + jnp.dot(lo, bd, preferred_element_type=F32))
        return z * lax.rsqrt(ss * (1.0 / B_HEAD_DIM) + EPS) * gain

    ug_ref[...] = jax.nn.gelu(seg(0)).astype(BF16)
    vn_ref[...] = _rms(jax.nn.gelu(seg(1)), gn_ref[...]).astype(BF16)
    q_ref[...] = (head_norm(seg(2), qg_ref[...]) * (B_HEAD_DIM ** -0.5)).astype(BF16)
    k_ref[...] = head_norm(seg(3), kg_ref[...]).astype(BF16)
    v_ref[...] = seg(4).astype(BF16)


def _mixer_kernel(ug_ref, vn_ref, q_ref, kp_ref, vp_ref, ws_ref, bst_ref, bias_ref,
                  wo_ref, x_ref, gf_ref, h_ref, hn_ref):
    qstart = pl.multiple_of(pl.program_id(1) * Q_TILE, Q_TILE)

    row = lax.broadcasted_iota(jnp.int32, (A_BLOCK, A_BLOCK), 0)
    colp = lax.broadcasted_iota(jnp.int32, (A_BLOCK, A_BLOCK), 1)
    causal = (colp // CHUNK) <= (row // CHUNK)
    low = colp < CHUNK
    vn = vn_ref[...].astype(F32)
    sp_parts = []
    for p in range(A_GROUPS // 2):
        vp = vn[:, p * LANES:(p + 1) * LANES]
        v_lo = jnp.where(low, vp, 0.0).astype(BF16)
        v_hi = jnp.where(low, 0.0, vp).astype(BF16)
        w0 = jnp.where(causal, ws_ref[2 * p], 0.0).astype(BF16)
        w1 = jnp.where(causal, ws_ref[2 * p + 1], 0.0).astype(BF16)
        sp = (jnp.dot(w0, v_lo, preferred_element_type=F32)
              + jnp.dot(w1, v_hi, preferred_element_type=F32))
        b = jnp.where(low, bst_ref[:, 2 * p:2 * p + 1], bst_ref[:, 2 * p + 1:2 * p + 2])
        sp_parts.append(sp + b)
    a_out = ug_ref[...].astype(F32) * jnp.concatenate(sp_parts, axis=1)

    q = q_ref[...]
    kw = kp_ref[0, pl.ds(qstart, K_WIN), :]
    vw = vp_ref[0, pl.ds(qstart, K_WIN), :]
    kcol = lax.broadcasted_iota(jnp.int32, (Q_TILE, K_WIN), 1)
    in_seq = (kcol + qstart) >= B_LEFT
    outs = []
    for hh in range(B_HEADS):
        sl = slice(hh * B_HEAD_DIM, (hh + 1) * B_HEAD_DIM)
        s = lax.dot_general(q[:, sl], kw[:, sl], (((1,), (1,)), ((), ())),
                            preferred_element_type=F32)
        s = jnp.where(in_seq, s + bias_ref[hh], NEG_INF)
        e = jnp.exp(s - jnp.max(s, axis=-1, keepdims=True))
        p_att = (e / jnp.sum(e, axis=-1, keepdims=True)).astype(BF16)
        outs.append(jnp.dot(p_att, vw[:, sl], preferred_element_type=F32))
    b_out = jnp.concatenate(outs, axis=1)

    mix = jnp.concatenate([a_out.astype(BF16), b_out.astype(BF16)], axis=1)
    h = x_ref[...] + jnp.dot(mix, wo_ref[...], preferred_element_type=F32)
    h_ref[...] = h
    hn_ref[...] = _rms(h, gf_ref[...])


def _top16_rows(sc, row_id, n_rows):
    vals, ids = [], []
    for _ in range(PEER_TOPK):
        m = jnp.max(sc, axis=0, keepdims=True)
        i = jnp.min(jnp.where(sc == m, row_id, n_rows), axis=0, keepdims=True)
        vals.append(m)
        ids.append(i)
        sc = jnp.where(row_id == i, -jnp.inf, sc)
    return jnp.concatenate(vals, axis=0), jnp.concatenate(ids, axis=0)


def _route_kernel(hn_ref, wq_ref, keys_ref, idx_ref, gate_ref):
    hb = hn_ref[...].astype(BF16)
    q = jnp.dot(hb, wq_ref[...], preferred_element_type=F32)
    key_id = lax.broadcasted_iota(jnp.int32, (PEER_NKEYS, TOK_TILE), 0)
    tops = []
    for p in range(2):
        qp = q[:, p * PEER_HALF:(p + 1) * PEER_HALF].astype(BF16)
        sc = lax.dot_general(keys_ref[0, p], qp, (((1,), (1,)), ((), ())),
                             preferred_element_type=F32)
        tops.append(_top16_rows(sc, key_id, PEER_NKEYS))
    (s1, i1), (s2, i2) = tops

    cand, flat, eid = [s1[0:1] + s2], [], [i1[0:1] * PEER_NKEYS + i2]
    r16 = lax.broadcasted_iota(jnp.int32, (PEER_TOPK, TOK_TILE), 0)
    r8 = lax.broadcasted_iota(jnp.int32, (8, TOK_TILE), 0)
    flat.append(r16)
    for a in range(1, 8):
        cand.append(s1[a:a + 1] + s2[0:8])
        flat.append(r8 + a * PEER_TOPK)
        eid.append(i1[a:a + 1] * PEER_NKEYS + i2[0:8])
    cand.append(s1[8:16] + s2[0:1])
    flat.append((r8 + 8) * PEER_TOPK)
    eid.append(i1[8:16] * PEER_NKEYS + i2[0:1])
    cand = jnp.concatenate(cand, axis=0)
    flat = jnp.concatenate(flat, axis=0)
    eid = jnp.concatenate(eid, axis=0)

    best, chosen = [], []
    for _ in range(PEER_TOPK):
        m = jnp.max(cand, axis=0, keepdims=True)
        f = jnp.min(jnp.where(cand == m, flat, PEER_TOPK * PEER_TOPK), axis=0, keepdims=True)
        sel = flat == f
        best.append(m)
        chosen.append(jnp.max(jnp.where(sel, eid, -1), axis=0, keepdims=True))
        cand = jnp.where(sel, -jnp.inf, cand)
    best = jnp.concatenate(best, axis=0)
    ex = jnp.exp(best - best[0:1])
    gate_ref[0] = ex / jnp.sum(ex, axis=0, keepdims=True)
    idx_ref[0] = jnp.concatenate(chosen, axis=0) * ROW_SUB


def _unpack_row(tbl_ref, row_start):
    w = tbl_ref[pl.ds(pl.multiple_of(row_start, ROW_SUB), ROW_SUB), :]
    lo = pltpu.bitcast(w << 16, F32)
    hi = pltpu.bitcast(w & jnp.uint32(0xFFFF0000), F32)
    return lo, hi


def _peer_u_kernel(idx_ref, x_ref, gate_ref, tbl_ref, c_ref, pbuf, hs):
    ones = jnp.ones((8, LANES), BF16)
    pbuf[...] = jnp.zeros_like(pbuf)

    def reduce_planes(dst_row):
        r = ((pbuf[0:PEER_PAIRS] + pbuf[PSTRIDE:PSTRIDE + PEER_PAIRS])
             + (pbuf[2 * PSTRIDE:2 * PSTRIDE + PEER_PAIRS]
                + pbuf[3 * PSTRIDE:3 * PSTRIDE + PEER_PAIRS]))
        hi, lo = _split_bf16(r)
        dn = (((1,), (1,)), ((), ()))
        tot = (lax.dot_general(ones, hi, dn, preferred_element_type=F32)
               + lax.dot_general(ones, lo, dn, preferred_element_type=F32))
        hs[pl.ds(dst_row, 1), :] = tot[0:1]

    def token_body(t, carry):
        reduce_planes(t + 7)
        x8 = x_ref[pl.ds(pl.multiple_of(t * 8, 8), 8), :]
        x_lo = x8[0:ROW_SUB]
        x_hi = x8[ROW_SUB:2 * ROW_SUB]
        for k in range(PEER_PAIRS):
            lo, hi = _unpack_row(tbl_ref, idx_ref[t * PEER_PAIRS + k])
            pbuf[pl.ds(k, ROW_SUB, stride=PSTRIDE), :] = lo * x_lo + hi * x_hi
        return carry

    lax.fori_loop(0, TOK_TILE, token_body, 0)
    reduce_planes(TOK_TILE + 7)
    c_ref[...] = gate_ref[...] * jax.nn.gelu(hs[8:8 + TOK_TILE])


def _peer_v_kernel(idx_ref, c_ref, h_ref, tbl_ref, out_ref):
    n_acc = 4

    def token_body(t, carry):
        acc_lo = [jnp.zeros((ROW_SUB, LANES), F32)] * n_acc
        acc_hi = [jnp.zeros((ROW_SUB, LANES), F32)] * n_acc
        for k in range(PEER_PAIRS):
            lo, hi = _unpack_row(tbl_ref, idx_ref[t * PEER_PAIRS + k])
            c = c_ref[t * PEER_PAIRS + k]
            acc_lo[k % n_acc] = acc_lo[k % n_acc] + c * lo
            acc_hi[k % n_acc] = acc_hi[k % n_acc] + c * hi
        lo = (acc_lo[0] + acc_lo[1]) + (acc_lo[2] + acc_lo[3])
        hi = (acc_hi[0] + acc_hi[1]) + (acc_hi[2] + acc_hi[3])
        rows = pl.ds(pl.multiple_of(t * 8, 8), 8)
        out_ref[rows, :] = h_ref[rows, :] + jnp.concatenate([lo, hi], axis=0)
        return carry

    lax.fori_loop(0, TOK_TILE, token_body, 0)


def _pack_table(tab):
    tb = tab.astype(BF16)
    lo = lax.bitcast_convert_type(tb[:, :ROW_WORDS], jnp.uint16).astype(jnp.uint32)
    hi = lax.bitcast_convert_type(tb[:, ROW_WORDS:], jnp.uint16).astype(jnp.uint32)
    return (lo | (hi << 16)).reshape(-1, LANES)


def _attn_bias(rel_bias):
    r = jnp.arange(Q_TILE)[:, None]
    j = jnp.arange(K_WIN)[None, :]
    rel_idx = jnp.clip(r + B_LEFT - j, -REL_CLIP, REL_CLIP) + REL_CLIP
    in_band = (j // CHUNK >= r // CHUNK) & (j // CHUNK <= r // CHUNK + B_LEFT // CHUNK)
    return jnp.where(in_band[None], rel_bias.astype(F32)[:, rel_idx], NEG_INF)


def _resident(shape):
    return pl.BlockSpec(shape, lambda *_: (0,) * len(shape))


def _layer(x2, bsz, seq, ln_mix_g, w_in, gmlp_norm_g, gmlp_w_s, gmlp_b_s, q_norm_g, k_norm_g,
           rel_bias, w_out, ln_ffn_g, peer_w_query, peer_sub_keys, peer_u, peer_v):
    T = bsz * seq
    row2 = lambda a: a.reshape(1, -1).astype(F32)

    bd = jnp.kron(jnp.eye(B_HEADS, dtype=F32), jnp.ones((B_HEAD_DIM, B_HEAD_DIM), F32)).astype(BF16)
    col_spec = pl.BlockSpec((PROJ_ROWS, MIX_A), lambda i: (i, 0))
    ug, vn, qn, kn, vv = pl.pallas_call(
        _proj_kernel,
        grid=(T // PROJ_ROWS,),
        in_specs=[pl.BlockSpec((PROJ_ROWS, D_MODEL), lambda i: (i, 0)),
                  _resident((1, D_MODEL)),
                  _resident((D_MODEL, 2 * MIX_A + 3 * MIX_B)),
                  _resident((1, MIX_A)), _resident((1, MIX_B)), _resident((1, MIX_B)),
                  _resident((MIX_B, MIX_B))],
        out_specs=[col_spec] * 5,
        out_shape=[jax.ShapeDtypeStruct((T, MIX_A), BF16)] * 5,
        compiler_params=pltpu.CompilerParams(dimension_semantics=("parallel",),
                                             vmem_limit_bytes=VMEM_LIMIT),
        name="proj",
    )(x2, row2(ln_mix_g), w_in.astype(BF16), row2(gmlp_norm_g),
      row2(jnp.tile(q_norm_g, B_HEADS)), row2(jnp.tile(k_norm_g, B_HEADS)), bd)

    pad_left = lambda a: jnp.pad(a.reshape(bsz, seq, MIX_B), ((0, 0), (B_LEFT, 0), (0, 0)))
    n_qt = seq // Q_TILE
    tile_spec = lambda w: pl.BlockSpec((Q_TILE, w), lambda b, i: (b * n_qt + i, 0))
    seq_spec = pl.BlockSpec((1, seq + B_LEFT, MIX_B), lambda b, i: (b, 0, 0))
    h, hn = pl.pallas_call(
        _mixer_kernel,
        grid=(bsz, n_qt),
        in_specs=[tile_spec(MIX_A), tile_spec(MIX_A), tile_spec(MIX_B), seq_spec, seq_spec,
                  _resident((A_GROUPS, A_BLOCK, A_BLOCK)), _resident((A_BLOCK, A_GROUPS)),
                  _resident((B_HEADS, Q_TILE, K_WIN)), _resident((D_MODEL, D_MODEL)),
                  tile_spec(D_MODEL), _resident((1, D_MODEL))],
        out_specs=[tile_spec(D_MODEL), tile_spec(D_MODEL)],
        out_shape=[jax.ShapeDtypeStruct((T, D_MODEL), F32)] * 2,
        compiler_params=pltpu.CompilerParams(dimension_semantics=("parallel", "parallel"),
                                             vmem_limit_bytes=VMEM_LIMIT),
        name="mixer",
    )(ug, vn, qn, pad_left(kn), pad_left(vv), gmlp_w_s.astype(F32), gmlp_b_s.astype(F32).T,
      _attn_bias(rel_bias), w_out.astype(BF16), x2, row2(ln_ffn_g))

    n_tt = T // TOK_TILE
    qw = 2 * PEER_HALF
    route_out = pl.BlockSpec((1, PEER_TOPK, TOK_TILE), lambda i, hh: (i, hh, 0))
    idx_t, gate_t = pl.pallas_call(
        _route_kernel,
        grid=(n_tt, PEER_HEADS),
        in_specs=[pl.BlockSpec((TOK_TILE, D_MODEL), lambda i, hh: (i, 0)),
                  pl.BlockSpec((D_MODEL, qw), lambda i, hh: (0, hh)),
                  pl.BlockSpec((1, 2, PEER_NKEYS, PEER_HALF), lambda i, hh: (hh, 0, 0, 0))],
        out_specs=[route_out, route_out],
        out_shape=[jax.ShapeDtypeStruct((n_tt, PEER_PAIRS, TOK_TILE), jnp.int32),
                   jax.ShapeDtypeStruct((n_tt, PEER_PAIRS, TOK_TILE), F32)],
        compiler_params=pltpu.CompilerParams(dimension_semantics=("parallel", "arbitrary"),
                                             vmem_limit_bytes=VMEM_LIMIT),
        name="route",
    )(hn, peer_w_query.astype(BF16), peer_sub_keys.astype(BF16))
    idx = jnp.swapaxes(idx_t, 1, 2).reshape(T * PEER_PAIRS)
    gate = jnp.swapaxes(gate_t, 1, 2).reshape(T, PEER_PAIRS)

    n_exp = peer_u.shape[0]
    tbl_spec = pl.BlockSpec((n_exp * ROW_SUB, LANES), lambda i: (0, 0),
                            pipeline_mode=pl.Buffered(1))
    idx_spec = pl.BlockSpec((TOK_TILE * PEER_PAIRS,), lambda i: (i,), memory_space=pltpu.SMEM)
    row_spec = pl.BlockSpec((TOK_TILE * 8, LANES), lambda i: (i, 0))
    pair_spec = pl.BlockSpec((TOK_TILE, PEER_PAIRS), lambda i: (i, 0))
    expert_params = pltpu.CompilerParams(dimension_semantics=("arbitrary",),
                                         vmem_limit_bytes=VMEM_LIMIT)
    coef = pl.pallas_call(
        _peer_u_kernel,
        grid=(n_tt,),
        in_specs=[idx_spec, row_spec, pair_spec, tbl_spec],
        out_specs=pair_spec,
        out_shape=jax.ShapeDtypeStruct((T, PEER_PAIRS), F32),
        scratch_shapes=[pltpu.VMEM((ROW_SUB * PSTRIDE, LANES), F32),
                        pltpu.VMEM((TOK_TILE + 8, PEER_PAIRS), F32)],
        compiler_params=expert_params,
        name="peer_u",
    )(idx, hn.reshape(T * 8, LANES), gate, _pack_table(peer_u))
    out = pl.pallas_call(
        _peer_v_kernel,
        grid=(n_tt,),
        in_specs=[idx_spec, idx_spec, row_spec, tbl_spec],
        out_specs=row_spec,
        out_shape=jax.ShapeDtypeStruct((T * 8, LANES), F32),
        compiler_params=expert_params,
        name="peer_v",
    )(idx, coef.reshape(T * PEER_PAIRS), h.reshape(T * 8, LANES), _pack_table(peer_v))
    return out.reshape(T, D_MODEL)


def kernel(x, ln_mix_g, w_in, gmlp_norm_g, gmlp_w_s, gmlp_b_s, q_norm_g, k_norm_g, rel_bias, w_out, ln_ffn_g, peer_w_query, peer_sub_keys, peer_u, peer_v):
    bsz, seq, d = x.shape
    x2 = x.reshape(bsz * seq, d)
    for l in range(ln_mix_g.shape[0]):
        x2 = _layer(x2, bsz, seq, ln_mix_g[l], w_in[l], gmlp_norm_g[l], gmlp_w_s[l], gmlp_b_s[l],
                    q_norm_g[l], k_norm_g[l], rel_bias[l], w_out[l], ln_ffn_g[l],
                    peer_w_query[l], peer_sub_keys[l], peer_u[l], peer_v[l])
    return x2.reshape(bsz, seq, d)
```

```python
import functools

import jax
import jax.numpy as jnp
from jax import lax
from jax.experimental import pallas as pl
from jax.experimental.pallas import tpu as pltpu

F32 = jnp.float32
BF16 = jnp.bfloat16

D_MODEL = 1024
CHUNK = 64
MIX_A = 512
MIX_B = 512
A_GROUPS = 8
A_BLOCK = 128
B_HEADS = 8
B_HEAD_DIM = 64
B_LEFT = 8 * CHUNK
REL_CLIP = 128
PEER_HEADS = 8
PEER_NKEYS = 128
PEER_HALF = 128
PEER_TOPK = 16
PEER_PAIRS = PEER_HEADS * PEER_TOPK
EPS = 1e-6
NEG_INF = -1e30

LANES = 128
SUBLANES = 8
ROUTE_HEADS = 4
Q_TILE = 128
K_WIN = B_LEFT + Q_TILE
PROJ_ROWS = 512
TOK_TILE = 128
ROW_WORDS = D_MODEL // 2
ROW_SUB = ROW_WORDS // LANES
PSTRIDE = PEER_PAIRS + 8
VMEM_LIMIT = 48 * 1024 * 1024


def _rms(x, g):
    return x * lax.rsqrt(jnp.mean(x * x, axis=-1, keepdims=True) + EPS) * g


def _split_bf16(x):
    hi = x.astype(BF16)
    lo = (x - hi.astype(F32)).astype(BF16)
    return hi, lo


def _proj_kernel(x_ref, g_ref, w_ref, gn_ref, qg_ref, kg_ref, bd_ref,
                 ug_ref, vn_ref, q_ref, k_ref, v_ref):
    xb = _rms(x_ref[...], g_ref[...]).astype(BF16)

    def seg(j):
        return jnp.dot(xb, w_ref[:, j * MIX_A:(j + 1) * MIX_A], preferred_element_type=F32)

    def head_norm(z, gain):
        hi, lo = _split_bf16(z * z)
        bd = bd_ref[...]
        ss = (jnp.dot(hi, bd, preferred_element_type=F32)
              + jnp.dot(lo, bd, preferred_element_type=F32))
        return z * lax.rsqrt(ss * (1.0 / B_HEAD_DIM) + EPS) * gain

    ug_ref[...] = jax.nn.gelu(seg(0)).astype(BF16)
    vn_ref[...] = _rms(jax.nn.gelu(seg(1)), gn_ref[...]).astype(BF16)
    q_ref[...] = (head_norm(seg(2), qg_ref[...]) * (B_HEAD_DIM ** -0.5)).astype(BF16)
    k_ref[...] = head_norm(seg(3), kg_ref[...]).astype(BF16)
    v_ref[...] = seg(4).astype(BF16)


def _mixer_kernel(ug_ref, vn_ref, q_ref, kp_ref, vp_ref, ws_ref, bst_ref, bias_ref,
                  wo_ref, x_ref, gf_ref, h_ref, hn_ref):
    qstart = pl.multiple_of(pl.program_id(1) * Q_TILE, Q_TILE)

    row = lax.broadcasted_iota(jnp.int32, (A_BLOCK, A_BLOCK), 0)
    colp = lax.broadcasted_iota(jnp.int32, (A_BLOCK, A_BLOCK), 1)
    causal = (colp // CHUNK) <= (row // CHUNK)
    low = colp < CHUNK
    vn = vn_ref[...].astype(F32)
    sp_parts = []
    for p in range(A_GROUPS // 2):
        vp = vn[:, p * LANES:(p + 1) * LANES]
        v_lo = jnp.where(low, vp, 0.0).astype(BF16)
        v_hi = jnp.where(low, 0.0, vp).astype(BF16)
        w0 = jnp.where(causal, ws_ref[2 * p], 0.0).astype(BF16)
        w1 = jnp.where(causal, ws_ref[2 * p + 1], 0.0).astype(BF16)
        sp = (jnp.dot(w0, v_lo, preferred_element_type=F32)
              + jnp.dot(w1, v_hi, preferred_element_type=F32))
        b = jnp.where(low, bst_ref[:, 2 * p:2 * p + 1], bst_ref[:, 2 * p + 1:2 * p + 2])
        sp_parts.append(sp + b)
    a_out = ug_ref[...].astype(F32) * jnp.concatenate(sp_parts, axis=1)

    q = q_ref[...]
    kw = kp_ref[0, pl.ds(qstart, K_WIN), :]
    vw = vp_ref[0, pl.ds(qstart, K_WIN), :]
    kcol = lax.broadcasted_iota(jnp.int32, (Q_TILE, K_WIN), 1)
    in_seq = (kcol + qstart) >= B_LEFT
    outs = []
    for hh in range(B_HEADS):
        sl = slice(hh * B_HEAD_DIM, (hh + 1) * B_HEAD_DIM)
        s = lax.dot_general(q[:, sl], kw[:, sl], (((1,), (1,)), ((), ())),
                            preferred_element_type=F32)
        s = jnp.where(in_seq, s + bias_ref[hh], NEG_INF)
        e = jnp.exp(s - jnp.max(s, axis=-1, keepdims=True))
        p_att = (e / jnp.sum(e, axis=-1, keepdims=True)).astype(BF16)
        outs.append(jnp.dot(p_att, vw[:, sl], preferred_element_type=F32))
    b_out = jnp.concatenate(outs, axis=1)

    mix = jnp.concatenate([a_out.astype(BF16), b_out.astype(BF16)], axis=1)
    h = x_ref[...] + jnp.dot(mix, wo_ref[...], preferred_element_type=F32)
    h_ref[...] = h
    hn_ref[...] = _rms(h, gf_ref[...])


def _top16_rows(val, order, *tags):
    n = val.shape[0]
    outs = [[] for _ in range(2 + len(tags))]
    for _ in range(PEER_TOPK):
        nodes = [tuple(a[i:i + SUBLANES] for a in (val, order) + tags)
                 for i in range(0, n, SUBLANES)]
        while len(nodes) > 1:
            merged = []
            for a, b in zip(nodes[0::2], nodes[1::2]):
                take = a[0] >= b[0]
                merged.append(tuple(jnp.where(take, x, y) for x, y in zip(a, b)))
            if len(nodes) % 2:
                merged.append(nodes[-1])
            nodes = merged
        v8, o8 = nodes[0][0], nodes[0][1]
        m = jnp.max(v8, axis=0, keepdims=True)
        o = jnp.min(jnp.where(v8 == m, o8, jnp.inf), axis=0, keepdims=True)
        outs[0].append(m)
        outs[1].append(o)
        for j, t8 in enumerate(nodes[0][2:]):
            outs[2 + j].append(jnp.max(jnp.where(o8 == o, t8, -1.0), axis=0, keepdims=True))
        val = jnp.where(order == o, -jnp.inf, val)
    return [jnp.concatenate(o, axis=0) for o in outs]


def _route_kernel(hn_ref, wq_ref, keys_ref, idx_ref, gate_ref):
    hb = hn_ref[...].astype(BF16)
    q = jnp.dot(hb, wq_ref[...], preferred_element_type=F32)
    key_id = lax.broadcasted_iota(jnp.int32, (PEER_NKEYS, TOK_TILE), 0).astype(F32)
    r16 = lax.broadcasted_iota(jnp.int32, (PEER_TOPK, TOK_TILE), 0).astype(F32)
    r8 = r16[0:SUBLANES]
    for hh in range(ROUTE_HEADS):
        tops = []
        for p in range(2):
            col = (2 * hh + p) * PEER_HALF
            qp = q[:, col:col + PEER_HALF].astype(BF16)
            sc = lax.dot_general(keys_ref[hh, p], qp, (((1,), (1,)), ((), ())),
                                 preferred_element_type=F32)
            tops.append(_top16_rows(sc, key_id))
        (s1, i1), (s2, i2) = tops

        cand, flat, eid = [s1[0:1] + s2], [r16], [i1[0:1] * PEER_NKEYS + i2]
        for a in range(1, 8):
            cand.append(s1[a:a + 1] + s2[0:8])
            flat.append(r8 + a * PEER_TOPK)
            eid.append(i1[a:a + 1] * PEER_NKEYS + i2[0:8])
        cand.append(s1[8:16] + s2[0:1])
        flat.append((r8 + 8) * PEER_TOPK)
        eid.append(i1[8:16] * PEER_NKEYS + i2[0:1])
        best, _, chosen = _top16_rows(jnp.concatenate(cand, axis=0),
                                      jnp.concatenate(flat, axis=0),
                                      jnp.concatenate(eid, axis=0))
        ex = jnp.exp(best - best[0:1])
        rows = slice(hh * PEER_TOPK, (hh + 1) * PEER_TOPK)
        gate_ref[0, rows, :] = ex / jnp.sum(ex, axis=0, keepdims=True)
        idx_ref[0, rows, :] = chosen.astype(jnp.int32) * ROW_SUB


N_CHUNK = D_MODEL // LANES
N_GBUF = 4


def _gather_rows(tbl_ref, idx_ref, gbuf, tok):
    tok_idx = idx_ref.at[pl.ds(tok * PEER_PAIRS, PEER_PAIRS)]
    for k in range(PEER_PAIRS):
        start = pl.multiple_of(tok_idx[k], ROW_SUB)
        gbuf[pl.ds(k, ROW_SUB, stride=PSTRIDE), :] = tbl_ref[pl.ds(start, ROW_SUB), :]


def _plane(gbuf, j):
    w = gbuf[j * PSTRIDE:j * PSTRIDE + PEER_PAIRS]
    return pltpu.bitcast(w << 16, F32), pltpu.bitcast(w & jnp.uint32(0xFFFF0000), F32)


def _pipelined_tokens(gather, bulk, gbufs):
    a, b, c, d = gbufs
    gather(a, 0)
    gather(b, 1)

    def four_tokens(j, carry):
        t0 = 4 * j
        gather(c, t0 + 2)
        gather(d, t0 + 3)
        bulk(a, t0)
        bulk(b, t0 + 1)
        gather(a, jnp.minimum(t0 + 4, TOK_TILE - 2))
        gather(b, jnp.minimum(t0 + 5, TOK_TILE - 1))
        bulk(c, t0 + 2)
        bulk(d, t0 + 3)
        return carry

    lax.fori_loop(0, TOK_TILE // 4, four_tokens, 0)


def _peer_u_kernel(idx_ref, x_ref, gate_ref, tbl_ref, c_ref, g0, g1, g2, g3, hs, xbuf):
    hs[...] = jnp.zeros_like(hs)
    lane = lax.broadcasted_iota(jnp.int32, (PEER_PAIRS, TOK_TILE), 1)
    for j in range(N_CHUNK):
        xbuf[pl.ds(j, TOK_TILE, stride=N_CHUNK), :] = x_ref[:, j * LANES:(j + 1) * LANES]

    def bulk(gbuf, tok):
        x8 = xbuf[pl.ds(pl.multiple_of(tok * N_CHUNK, N_CHUNK), N_CHUNK), :]
        acc = None
        for j in range(ROW_SUB):
            lo, hi = _plane(gbuf, j)
            term = lo * x8[j:j + 1] + hi * x8[ROW_SUB + j:ROW_SUB + j + 1]
            acc = term if acc is None else acc + term
        col = jnp.sum(acc, axis=1, keepdims=True)
        hs[...] = jnp.where(lane == tok, col, hs[...])

    _pipelined_tokens(functools.partial(_gather_rows, tbl_ref, idx_ref), bulk, (g0, g1, g2, g3))
    c_ref[0] = gate_ref[0] * jax.nn.gelu(hs[...])


def _peer_v_kernel(idx_ref, ct_ref, h_ref, tbl_ref, out_ref, g0, g1, g2, g3, c_hi, c_lo, obuf):
    hi, lo = _split_bf16(ct_ref[0])
    c_hi[...] = hi
    c_lo[...] = lo
    tok_row = lax.broadcasted_iota(jnp.int32, (TOK_TILE, LANES), 0)

    def bulk(gbuf, tok):
        sel = jnp.where(tok_row == tok, 1.0, 0.0).astype(BF16)
        cm = (jnp.dot(c_hi[...], sel, preferred_element_type=F32)
              + jnp.dot(c_lo[...], sel, preferred_element_type=F32))
        sums = [None] * N_CHUNK
        for j in range(ROW_SUB):
            lo, hi = _plane(gbuf, j)
            sums[j] = jnp.sum(lo * cm, axis=0, keepdims=True)
            sums[ROW_SUB + j] = jnp.sum(hi * cm, axis=0, keepdims=True)
        rows = pl.ds(pl.multiple_of(tok * N_CHUNK, N_CHUNK), N_CHUNK)
        obuf[rows, :] = jnp.concatenate(sums, axis=0)

    _pipelined_tokens(functools.partial(_gather_rows, tbl_ref, idx_ref), bulk, (g0, g1, g2, g3))
    for j in range(N_CHUNK):
        cols = slice(j * LANES, (j + 1) * LANES)
        out_ref[:, cols] = h_ref[:, cols] + obuf[pl.ds(j, TOK_TILE, stride=N_CHUNK), :]


def _pack_table(tab):
    tb = tab.astype(BF16)
    lo = lax.bitcast_convert_type(tb[:, :ROW_WORDS], jnp.uint16).astype(jnp.uint32)
    hi = lax.bitcast_convert_type(tb[:, ROW_WORDS:], jnp.uint16).astype(jnp.uint32)
    return (lo | (hi << 16)).reshape(-1, LANES)


def _attn_bias(rel_bias):
    r = jnp.arange(Q_TILE)[:, None]
    j = jnp.arange(K_WIN)[None, :]
    in_band = (j // CHUNK >= r // CHUNK) & (j // CHUNK <= r // CHUNK + B_LEFT // CHUNK)
    rb = rel_bias.astype(F32)
    period = K_WIN + Q_TILE - 1
    n_far = period - 2 * REL_CLIP + 1
    v = jnp.concatenate([jnp.broadcast_to(rb[:, 2 * REL_CLIP:], (B_HEADS, n_far)),
                         rb[:, 2 * REL_CLIP - 1:0:-1]], axis=1)
    v = jnp.roll(v, -(Q_TILE - 1), axis=1)
    skew = jnp.tile(v, (1, Q_TILE + 1))[:, :Q_TILE * (period - 1)]
    bias = skew.reshape(B_HEADS, Q_TILE, period - 1)[:, :, :K_WIN]
    return jnp.where(in_band[None], bias, NEG_INF)


def _resident(shape):
    return pl.BlockSpec(shape, lambda *_: (0,) * len(shape))


def _layer(x2, bsz, seq, ln_mix_g, w_in, gmlp_norm_g, gmlp_w_s, gmlp_b_s, q_norm_g, k_norm_g,
           rel_bias, w_out, ln_ffn_g, peer_w_query, peer_sub_keys, peer_u, peer_v):
    T = bsz * seq
    row2 = lambda a: a.reshape(1, -1).astype(F32)

    bd = jnp.kron(jnp.eye(B_HEADS, dtype=F32), jnp.ones((B_HEAD_DIM, B_HEAD_DIM), F32)).astype(BF16)
    col_spec = pl.BlockSpec((PROJ_ROWS, MIX_A), lambda i: (i, 0))
    ug, vn, qn, kn, vv = pl.pallas_call(
        _proj_kernel,
        grid=(T // PROJ_ROWS,),
        in_specs=[pl.BlockSpec((PROJ_ROWS, D_MODEL), lambda i: (i, 0)),
                  _resident((1, D_MODEL)),
                  _resident((D_MODEL, 2 * MIX_A + 3 * MIX_B)),
                  _resident((1, MIX_A)), _resident((1, MIX_B)), _resident((1, MIX_B)),
                  _resident((MIX_B, MIX_B))],
        out_specs=[col_spec] * 5,
        out_shape=[jax.ShapeDtypeStruct((T, MIX_A), BF16)] * 5,
        compiler_params=pltpu.CompilerParams(dimension_semantics=("parallel",),
                                             vmem_limit_bytes=VMEM_LIMIT),
        name="proj",
    )(x2, row2(ln_mix_g), w_in.astype(BF16), row2(gmlp_norm_g),
      row2(jnp.tile(q_norm_g, B_HEADS)), row2(jnp.tile(k_norm_g, B_HEADS)), bd)

    pad_left = lambda a: jnp.pad(a.reshape(bsz, seq, MIX_B), ((0, 0), (B_LEFT, 0), (0, 0)))
    n_qt = seq // Q_TILE
    tile_spec = lambda w: pl.BlockSpec((Q_TILE, w), lambda b, i: (b * n_qt + i, 0))
    seq_spec = pl.BlockSpec((1, seq + B_LEFT, MIX_B), lambda b, i: (b, 0, 0))
    h, hn = pl.pallas_call(
        _mixer_kernel,
        grid=(bsz, n_qt),
        in_specs=[tile_spec(MIX_A), tile_spec(MIX_A), tile_spec(MIX_B), seq_spec, seq_spec,
                  _resident((A_GROUPS, A_BLOCK, A_BLOCK)), _resident((A_BLOCK, A_GROUPS)),
                  _resident((B_HEADS, Q_TILE, K_WIN)), _resident((D_MODEL, D_MODEL)),
                  tile_spec(D_MODEL), _resident((1, D_MODEL))],
        out_specs=[tile_spec(D_MODEL), tile_spec(D_MODEL)],
        out_shape=[jax.ShapeDtypeStruct((T, D_MODEL), F32)] * 2,
        compiler_params=pltpu.CompilerParams(dimension_semantics=("parallel", "parallel"),
                                             vmem_limit_bytes=VMEM_LIMIT),
        name="mixer",
    )(ug, vn, qn, pad_left(kn), pad_left(vv), gmlp_w_s.astype(F32), gmlp_b_s.astype(F32).T,
      _attn_bias(rel_bias), w_out.astype(BF16), x2, row2(ln_ffn_g))

    n_tt = T // TOK_TILE
    qw = ROUTE_HEADS * 2 * PEER_HALF
    route_out = pl.BlockSpec((1, ROUTE_HEADS * PEER_TOPK, TOK_TILE), lambda i, hh: (i, hh, 0))
    idx_t, gate_t = pl.pallas_call(
        _route_kernel,
        grid=(n_tt, PEER_HEADS // ROUTE_HEADS),
        in_specs=[pl.BlockSpec((TOK_TILE, D_MODEL), lambda i, hh: (i, 0)),
                  pl.BlockSpec((D_MODEL, qw), lambda i, hh: (0, hh)),
                  pl.BlockSpec((ROUTE_HEADS, 2, PEER_NKEYS, PEER_HALF),
                               lambda i, hh: (hh, 0, 0, 0))],
        out_specs=[route_out, route_out],
        out_shape=[jax.ShapeDtypeStruct((n_tt, PEER_PAIRS, TOK_TILE), jnp.int32),
                   jax.ShapeDtypeStruct((n_tt, PEER_PAIRS, TOK_TILE), F32)],
        compiler_params=pltpu.CompilerParams(dimension_semantics=("parallel", "arbitrary"),
                                             vmem_limit_bytes=VMEM_LIMIT),
        name="route",
    )(hn, peer_w_query.astype(BF16), peer_sub_keys.astype(BF16))
    idx = jnp.swapaxes(idx_t, 1, 2).reshape(T * PEER_PAIRS)

    n_exp = peer_u.shape[0]
    tbl_spec = pl.BlockSpec((n_exp * ROW_SUB, LANES), lambda i: (0, 0),
                            pipeline_mode=pl.Buffered(1))
    idx_spec = pl.BlockSpec((TOK_TILE * PEER_PAIRS,), lambda i: (i,), memory_space=pltpu.SMEM)
    row_spec = pl.BlockSpec((TOK_TILE, D_MODEL), lambda i: (i, 0))
    pair_spec = pl.BlockSpec((1, PEER_PAIRS, TOK_TILE), lambda i: (i, 0, 0))
    expert_params = pltpu.CompilerParams(dimension_semantics=("arbitrary",),
                                         vmem_limit_bytes=VMEM_LIMIT)
    gather_bufs = [pltpu.VMEM((ROW_SUB * PSTRIDE, LANES), jnp.uint32)] * N_GBUF
    pair_buf = lambda dt: pltpu.VMEM((PEER_PAIRS, TOK_TILE), dt)
    chunk_buf = pltpu.VMEM((TOK_TILE * N_CHUNK, LANES), F32)
    coef_t = pl.pallas_call(
        _peer_u_kernel,
        grid=(n_tt,),
        in_specs=[idx_spec, row_spec, pair_spec, tbl_spec],
        out_specs=pair_spec,
        out_shape=jax.ShapeDtypeStruct((n_tt, PEER_PAIRS, TOK_TILE), F32),
        scratch_shapes=gather_bufs + [pair_buf(F32), chunk_buf],
        compiler_params=expert_params,
        name="peer_u",
    )(idx, hn, gate_t, _pack_table(peer_u))
    return pl.pallas_call(
        _peer_v_kernel,
        grid=(n_tt,),
        in_specs=[idx_spec, pair_spec, row_spec, tbl_spec],
        out_specs=row_spec,
        out_shape=jax.ShapeDtypeStruct((T, D_MODEL), F32),
        scratch_shapes=gather_bufs + [pair_buf(BF16), pair_buf(BF16), chunk_buf],
        compiler_params=expert_params,
        name="peer_v",
    )(idx, coef_t, h, _pack_table(peer_v))


def kernel(x, ln_mix_g, w_in, gmlp_norm_g, gmlp_w_s, gmlp_b_s, q_norm_g, k_norm_g, rel_bias, w_out, ln_ffn_g, peer_w_query, peer_sub_keys, peer_u, peer_v):
    bsz, seq, d = x.shape
    x2 = x.reshape(bsz * seq, d)
    for l in range(ln_mix_g.shape[0]):
        x2 = _layer(x2, bsz, seq, ln_mix_g[l], w_in[l], gmlp_norm_g[l], gmlp_w_s[l], gmlp_b_s[l],
                    q_norm_g[l], k_norm_g[l], rel_bias[l], w_out[l], ln_ffn_g[l],
                    peer_w_query[l], peer_sub_keys[l], peer_u[l], peer_v[l])
    return x2.reshape(bsz, seq, d)
```

```python
import functools

import jax
import jax.numpy as jnp
from jax import lax
from jax.experimental import pallas as pl
from jax.experimental.pallas import tpu as pltpu

F32 = jnp.float32
BF16 = jnp.bfloat16

D_MODEL = 1024
CHUNK = 64
MIX_A = 512
MIX_B = 512
A_GROUPS = 8
A_BLOCK = 128
B_HEADS = 8
B_HEAD_DIM = 64
B_LEFT = 8 * CHUNK
REL_CLIP = 128
PEER_HEADS = 8
PEER_NKEYS = 128
PEER_HALF = 128
PEER_TOPK = 16
PEER_PAIRS = PEER_HEADS * PEER_TOPK
EPS = 1e-6
NEG_INF = -1e30

LANES = 128
SUBLANES = 8
ROUTE_HEADS = 4
Q_TILE = 128
K_WIN = B_LEFT + Q_TILE
PROJ_ROWS = 512
TOK_TILE = 128
ROW_WORDS = D_MODEL // 2
ROW_SUB = ROW_WORDS // LANES
PSTRIDE = PEER_PAIRS + 8
VMEM_LIMIT = 48 * 1024 * 1024


def _rms(x, g):
    return x * lax.rsqrt(jnp.mean(x * x, axis=-1, keepdims=True) + EPS) * g


def _split_bf16(x):
    hi = x.astype(BF16)
    lo = (x - hi.astype(F32)).astype(BF16)
    return hi, lo


def _proj_kernel(x_ref, g_ref, w_ref, gn_ref, qg_ref, kg_ref, bd_ref,
                 ug_ref, vn_ref, q_ref, k_ref, v_ref):
    xb = _rms(x_ref[...], g_ref[...]).astype(BF16)

    def seg(j):
        return jnp.dot(xb, w_ref[:, j * MIX_A:(j + 1) * MIX_A], preferred_element_type=F32)

    def head_norm(z, gain):
        hi, lo = _split_bf16(z * z)
        bd = bd_ref[...]
        ss = (jnp.dot(hi, bd, preferred_element_type=F32)
              + jnp.dot(lo, bd, preferred_element_type=F32))
        return z * lax.rsqrt(ss * (1.0 / B_HEAD_DIM) + EPS) * gain

    ug_ref[...] = jax.nn.gelu(seg(0)).astype(BF16)
    vn_ref[...] = _rms(jax.nn.gelu(seg(1)), gn_ref[...]).astype(BF16)
    q_ref[...] = (head_norm(seg(2), qg_ref[...]) * (B_HEAD_DIM ** -0.5)).astype(BF16)
    k_ref[...] = head_norm(seg(3), kg_ref[...]).astype(BF16)
    v_ref[...] = seg(4).astype(BF16)


def _mixer_kernel(ug_ref, vn_ref, q_ref, kp_ref, vp_ref, ws_ref, bst_ref, bias_ref,
                  wo_ref, x_ref, gf_ref, h_ref, hn_ref):
    qstart = pl.multiple_of(pl.program_id(1) * Q_TILE, Q_TILE)

    row = lax.broadcasted_iota(jnp.int32, (A_BLOCK, A_BLOCK), 0)
    colp = lax.broadcasted_iota(jnp.int32, (A_BLOCK, A_BLOCK), 1)
    causal = (colp // CHUNK) <= (row // CHUNK)
    low = colp < CHUNK
    vn = vn_ref[...].astype(F32)
    sp_parts = []
    for p in range(A_GROUPS // 2):
        vp = vn[:, p * LANES:(p + 1) * LANES]
        v_lo = jnp.where(low, vp, 0.0).astype(BF16)
        v_hi = jnp.where(low, 0.0, vp).astype(BF16)
        w0 = jnp.where(causal, ws_ref[2 * p], 0.0).astype(BF16)
        w1 = jnp.where(causal, ws_ref[2 * p + 1], 0.0).astype(BF16)
        sp = (jnp.dot(w0, v_lo, preferred_element_type=F32)
              + jnp.dot(w1, v_hi, preferred_element_type=F32))
        b = jnp.where(low, bst_ref[:, 2 * p:2 * p + 1], bst_ref[:, 2 * p + 1:2 * p + 2])
        sp_parts.append(sp + b)
    a_out = ug_ref[...].astype(F32) * jnp.concatenate(sp_parts, axis=1)

    q = q_ref[...]
    kw = kp_ref[0, pl.ds(qstart, K_WIN), :]
    vw = vp_ref[0, pl.ds(qstart, K_WIN), :]
    kcol = lax.broadcasted_iota(jnp.int32, (Q_TILE, K_WIN), 1)
    in_seq = (kcol + qstart) >= B_LEFT
    outs = []
    for hh in range(B_HEADS):
        sl = slice(hh * B_HEAD_DIM, (hh + 1) * B_HEAD_DIM)
        s = lax.dot_general(q[:, sl], kw[:, sl], (((1,), (1,)), ((), ())),
                            preferred_element_type=F32)
        s = jnp.where(in_seq, s + bias_ref[hh], NEG_INF)
        e = jnp.exp(s - jnp.max(s, axis=-1, keepdims=True))
        p_att = (e / jnp.sum(e, axis=-1, keepdims=True)).astype(BF16)
        outs.append(jnp.dot(p_att, vw[:, sl], preferred_element_type=F32))
    b_out = jnp.concatenate(outs, axis=1)

    mix = jnp.concatenate([a_out.astype(BF16), b_out.astype(BF16)], axis=1)
    h = x_ref[...] + jnp.dot(mix, wo_ref[...], preferred_element_type=F32)
    h_ref[...] = h
    hn_ref[...] = _rms(h, gf_ref[...])


def _top16_rows(val, order, *tags):
    n = val.shape[0]
    outs = [[] for _ in range(2 + len(tags))]
    for _ in range(PEER_TOPK):
        nodes = [tuple(a[i:i + SUBLANES] for a in (val, order) + tags)
                 for i in range(0, n, SUBLANES)]
        while len(nodes) > 1:
            merged = []
            for a, b in zip(nodes[0::2], nodes[1::2]):
                take = a[0] >= b[0]
                merged.append(tuple(jnp.where(take, x, y) for x, y in zip(a, b)))
            if len(nodes) % 2:
                merged.append(nodes[-1])
            nodes = merged
        v8, o8 = nodes[0][0], nodes[0][1]
        m = jnp.max(v8, axis=0, keepdims=True)
        o = jnp.min(jnp.where(v8 == m, o8, jnp.inf), axis=0, keepdims=True)
        outs[0].append(m)
        outs[1].append(o)
        for j, t8 in enumerate(nodes[0][2:]):
            outs[2 + j].append(jnp.max(jnp.where(o8 == o, t8, -1.0), axis=0, keepdims=True))
        val = jnp.where(order == o, -jnp.inf, val)
    return [jnp.concatenate(o, axis=0) for o in outs]


def _route_kernel(hn_ref, wq_ref, keys_ref, idx_ref, gate_ref):
    hb = hn_ref[...].astype(BF16)
    q = jnp.dot(hb, wq_ref[...], preferred_element_type=F32)
    key_id = lax.broadcasted_iota(jnp.int32, (PEER_NKEYS, TOK_TILE), 0).astype(F32)
    r16 = lax.broadcasted_iota(jnp.int32, (PEER_TOPK, TOK_TILE), 0).astype(F32)
    r8 = r16[0:SUBLANES]
    for hh in range(ROUTE_HEADS):
        tops = []
        for p in range(2):
            col = (2 * hh + p) * PEER_HALF
            qp = q[:, col:col + PEER_HALF].astype(BF16)
            sc = lax.dot_general(keys_ref[hh, p], qp, (((1,), (1,)), ((), ())),
                                 preferred_element_type=F32)
            tops.append(_top16_rows(sc, key_id))
        (s1, i1), (s2, i2) = tops

        cand, flat, eid = [s1[0:1] + s2], [r16], [i1[0:1] * PEER_NKEYS + i2]
        for a in range(1, 8):
            cand.append(s1[a:a + 1] + s2[0:8])
            flat.append(r8 + a * PEER_TOPK)
            eid.append(i1[a:a + 1] * PEER_NKEYS + i2[0:8])
        cand.append(s1[8:16] + s2[0:1])
        flat.append((r8 + 8) * PEER_TOPK)
        eid.append(i1[8:16] * PEER_NKEYS + i2[0:1])
        best, _, chosen = _top16_rows(jnp.concatenate(cand, axis=0),
                                      jnp.concatenate(flat, axis=0),
                                      jnp.concatenate(eid, axis=0))
        ex = jnp.exp(best - best[0:1])
        rows = slice(hh * PEER_TOPK, (hh + 1) * PEER_TOPK)
        gate_ref[0, rows, :] = ex / jnp.sum(ex, axis=0, keepdims=True)
        idx_ref[0, rows, :] = chosen.astype(jnp.int32) * ROW_SUB


N_CHUNK = D_MODEL // LANES
GROUP = 8
N_GBUF = 2 * GROUP


def _gather_group(tbl_ref, idx_ref, gbufs, tok0):
    toks = [tok0 + i for i in range(GROUP)]
    for k in range(PEER_PAIRS):
        pair_idx = idx_ref.at[0, k]
        for i in range(GROUP):
            start = pl.multiple_of(pair_idx[toks[i]], ROW_SUB)
            gbufs[i][pl.ds(k, ROW_SUB, stride=PSTRIDE), :] = tbl_ref[pl.ds(start, ROW_SUB), :]


def _plane(gbuf, j):
    w = gbuf[j * PSTRIDE:j * PSTRIDE + PEER_PAIRS]
    return pltpu.bitcast(w << 16, F32), pltpu.bitcast(w & jnp.uint32(0xFFFF0000), F32)


def _pipelined_tokens(gather, bulk, gbufs):
    set0, set1 = gbufs[:GROUP], gbufs[GROUP:]
    gather(set0, 0)

    def two_groups(j, carry):
        t0 = 2 * GROUP * j
        gather(set1, t0 + GROUP)
        for i in range(GROUP):
            bulk(set0[i], t0 + i)
        gather(set0, jnp.minimum(t0 + 2 * GROUP, TOK_TILE - GROUP))
        for i in range(GROUP):
            bulk(set1[i], t0 + GROUP + i)
        return carry

    lax.fori_loop(0, TOK_TILE // (2 * GROUP), two_groups, 0)


def _peer_u_kernel(idx_ref, x_ref, gate_ref, tbl_ref, c_ref, *scratch):
    gbufs, (hs, xbuf) = scratch[:N_GBUF], scratch[N_GBUF:]
    hs[...] = jnp.zeros_like(hs)
    lane = lax.broadcasted_iota(jnp.int32, (PEER_PAIRS, TOK_TILE), 1)
    for j in range(N_CHUNK):
        xbuf[pl.ds(j, TOK_TILE, stride=N_CHUNK), :] = x_ref[:, j * LANES:(j + 1) * LANES]

    def bulk(gbuf, tok):
        x8 = xbuf[pl.ds(pl.multiple_of(tok * N_CHUNK, N_CHUNK), N_CHUNK), :]
        acc = None
        for j in range(ROW_SUB):
            lo, hi = _plane(gbuf, j)
            term = lo * x8[j:j + 1] + hi * x8[ROW_SUB + j:ROW_SUB + j + 1]
            acc = term if acc is None else acc + term
        col = jnp.sum(acc, axis=1, keepdims=True)
        hs[...] = jnp.where(lane == tok, col, hs[...])

    _pipelined_tokens(functools.partial(_gather_group, tbl_ref, idx_ref), bulk, gbufs)
    c_ref[0] = gate_ref[0] * jax.nn.gelu(hs[...])


def _peer_v_kernel(idx_ref, ct_ref, h_ref, tbl_ref, out_ref, *scratch):
    gbufs, (c_hi, c_lo, obuf) = scratch[:N_GBUF], scratch[N_GBUF:]
    hi, lo = _split_bf16(ct_ref[0])
    c_hi[...] = hi
    c_lo[...] = lo
    tok_row = lax.broadcasted_iota(jnp.int32, (TOK_TILE, LANES), 0)

    def bulk(gbuf, tok):
        sel = jnp.where(tok_row == tok, 1.0, 0.0).astype(BF16)
        cm = (jnp.dot(c_hi[...], sel, preferred_element_type=F32)
              + jnp.dot(c_lo[...], sel, preferred_element_type=F32))
        sums = [None] * N_CHUNK
        for j in range(ROW_SUB):
            lo, hi = _plane(gbuf, j)
            sums[j] = jnp.sum(lo * cm, axis=0, keepdims=True)
            sums[ROW_SUB + j] = jnp.sum(hi * cm, axis=0, keepdims=True)
        rows = pl.ds(pl.multiple_of(tok * N_CHUNK, N_CHUNK), N_CHUNK)
        obuf[rows, :] = jnp.concatenate(sums, axis=0)

    _pipelined_tokens(functools.partial(_gather_group, tbl_ref, idx_ref), bulk, gbufs)
    for j in range(N_CHUNK):
        cols = slice(j * LANES, (j + 1) * LANES)
        out_ref[:, cols] = h_ref[:, cols] + obuf[pl.ds(j, TOK_TILE, stride=N_CHUNK), :]


def _pack_table(tab):
    tb = tab.astype(BF16)
    lo = lax.bitcast_convert_type(tb[:, :ROW_WORDS], jnp.uint16).astype(jnp.uint32)
    hi = lax.bitcast_convert_type(tb[:, ROW_WORDS:], jnp.uint16).astype(jnp.uint32)
    return (lo | (hi << 16)).reshape(-1, LANES)


def _attn_bias(rel_bias):
    r = jnp.arange(Q_TILE)[:, None]
    j = jnp.arange(K_WIN)[None, :]
    in_band = (j // CHUNK >= r // CHUNK) & (j // CHUNK <= r // CHUNK + B_LEFT // CHUNK)
    rb = rel_bias.astype(F32)
    period = K_WIN + Q_TILE - 1
    n_far = period - 2 * REL_CLIP + 1
    v = jnp.concatenate([jnp.broadcast_to(rb[:, 2 * REL_CLIP:], (B_HEADS, n_far)),
                         rb[:, 2 * REL_CLIP - 1:0:-1]], axis=1)
    v = jnp.roll(v, -(Q_TILE - 1), axis=1)
    skew = jnp.tile(v, (1, Q_TILE + 1))[:, :Q_TILE * (period - 1)]
    bias = skew.reshape(B_HEADS, Q_TILE, period - 1)[:, :, :K_WIN]
    return jnp.where(in_band[None], bias, NEG_INF)


def _resident(shape):
    return pl.BlockSpec(shape, lambda *_: (0,) * len(shape))


def _layer(x2, bsz, seq, ln_mix_g, w_in, gmlp_norm_g, gmlp_w_s, gmlp_b_s, q_norm_g, k_norm_g,
           rel_bias, w_out, ln_ffn_g, peer_w_query, peer_sub_keys, peer_u, peer_v):
    T = bsz * seq
    row2 = lambda a: a.reshape(1, -1).astype(F32)

    bd = jnp.kron(jnp.eye(B_HEADS, dtype=F32), jnp.ones((B_HEAD_DIM, B_HEAD_DIM), F32)).astype(BF16)
    col_spec = pl.BlockSpec((PROJ_ROWS, MIX_A), lambda i: (i, 0))
    ug, vn, qn, kn, vv = pl.pallas_call(
        _proj_kernel,
        grid=(T // PROJ_ROWS,),
        in_specs=[pl.BlockSpec((PROJ_ROWS, D_MODEL), lambda i: (i, 0)),
                  _resident((1, D_MODEL)),
                  _resident((D_MODEL, 2 * MIX_A + 3 * MIX_B)),
                  _resident((1, MIX_A)), _resident((1, MIX_B)), _resident((1, MIX_B)),
                  _resident((MIX_B, MIX_B))],
        out_specs=[col_spec] * 5,
        out_shape=[jax.ShapeDtypeStruct((T, MIX_A), BF16)] * 5,
        compiler_params=pltpu.CompilerParams(dimension_semantics=("parallel",),
                                             vmem_limit_bytes=VMEM_LIMIT),
        name="proj",
    )(x2, row2(ln_mix_g), w_in.astype(BF16), row2(gmlp_norm_g),
      row2(jnp.tile(q_norm_g, B_HEADS)), row2(jnp.tile(k_norm_g, B_HEADS)), bd)

    pad_left = lambda a: jnp.pad(a.reshape(bsz, seq, MIX_B), ((0, 0), (B_LEFT, 0), (0, 0)))
    n_qt = seq // Q_TILE
    tile_spec = lambda w: pl.BlockSpec((Q_TILE, w), lambda b, i: (b * n_qt + i, 0))
    seq_spec = pl.BlockSpec((1, seq + B_LEFT, MIX_B), lambda b, i: (b, 0, 0))
    h, hn = pl.pallas_call(
        _mixer_kernel,
        grid=(bsz, n_qt),
        in_specs=[tile_spec(MIX_A), tile_spec(MIX_A), tile_spec(MIX_B), seq_spec, seq_spec,
                  _resident((A_GROUPS, A_BLOCK, A_BLOCK)), _resident((A_BLOCK, A_GROUPS)),
                  _resident((B_HEADS, Q_TILE, K_WIN)), _resident((D_MODEL, D_MODEL)),
                  tile_spec(D_MODEL), _resident((1, D_MODEL))],
        out_specs=[tile_spec(D_MODEL), tile_spec(D_MODEL)],
        out_shape=[jax.ShapeDtypeStruct((T, D_MODEL), F32)] * 2,
        compiler_params=pltpu.CompilerParams(dimension_semantics=("parallel", "parallel"),
                                             vmem_limit_bytes=VMEM_LIMIT),
        name="mixer",
    )(ug, vn, qn, pad_left(kn), pad_left(vv), gmlp_w_s.astype(F32), gmlp_b_s.astype(F32).T,
      _attn_bias(rel_bias), w_out.astype(BF16), x2, row2(ln_ffn_g))

    n_tt = T // TOK_TILE
    qw = ROUTE_HEADS * 2 * PEER_HALF
    route_out = pl.BlockSpec((1, ROUTE_HEADS * PEER_TOPK, TOK_TILE), lambda i, hh: (i, hh, 0))
    idx_t, gate_t = pl.pallas_call(
        _route_kernel,
        grid=(n_tt, PEER_HEADS // ROUTE_HEADS),
        in_specs=[pl.BlockSpec((TOK_TILE, D_MODEL), lambda i, hh: (i, 0)),
                  pl.BlockSpec((D_MODEL, qw), lambda i, hh: (0, hh)),
                  pl.BlockSpec((ROUTE_HEADS, 2, PEER_NKEYS, PEER_HALF),
                               lambda i, hh: (hh, 0, 0, 0))],
        out_specs=[route_out, route_out],
        out_shape=[jax.ShapeDtypeStruct((n_tt, PEER_PAIRS, TOK_TILE), jnp.int32),
                   jax.ShapeDtypeStruct((n_tt, PEER_PAIRS, TOK_TILE), F32)],
        compiler_params=pltpu.CompilerParams(dimension_semantics=("parallel", "arbitrary"),
                                             vmem_limit_bytes=VMEM_LIMIT),
        name="route",
    )(hn, peer_w_query.astype(BF16), peer_sub_keys.astype(BF16))

    n_exp = peer_u.shape[0]
    tbl_spec = pl.BlockSpec((n_exp * ROW_SUB, LANES), lambda i: (0, 0),
                            pipeline_mode=pl.Buffered(1))
    idx_spec = pl.BlockSpec((1, PEER_PAIRS, TOK_TILE), lambda i: (i, 0, 0),
                            memory_space=pltpu.SMEM, pipeline_mode=pl.Buffered(1))
    row_spec = pl.BlockSpec((TOK_TILE, D_MODEL), lambda i: (i, 0))
    pair_spec = pl.BlockSpec((1, PEER_PAIRS, TOK_TILE), lambda i: (i, 0, 0))
    expert_params = pltpu.CompilerParams(dimension_semantics=("arbitrary",),
                                         vmem_limit_bytes=VMEM_LIMIT)
    gather_bufs = [pltpu.VMEM((ROW_SUB * PSTRIDE, LANES), jnp.uint32)] * N_GBUF
    pair_buf = lambda dt: pltpu.VMEM((PEER_PAIRS, TOK_TILE), dt)
    chunk_buf = pltpu.VMEM((TOK_TILE * N_CHUNK, LANES), F32)
    coef_t = pl.pallas_call(
        _peer_u_kernel,
        grid=(n_tt,),
        in_specs=[idx_spec, row_spec, pair_spec, tbl_spec],
        out_specs=pair_spec,
        out_shape=jax.ShapeDtypeStruct((n_tt, PEER_PAIRS, TOK_TILE), F32),
        scratch_shapes=gather_bufs + [pair_buf(F32), chunk_buf],
        compiler_params=expert_params,
        name="peer_u",
    )(idx_t, hn, gate_t, _pack_table(peer_u))
    return pl.pallas_call(
        _peer_v_kernel,
        grid=(n_tt,),
        in_specs=[idx_spec, pair_spec, row_spec, tbl_spec],
        out_specs=row_spec,
        out_shape=jax.ShapeDtypeStruct((T, D_MODEL), F32),
        scratch_shapes=gather_bufs + [pair_buf(BF16), pair_buf(BF16), chunk_buf],
        compiler_params=expert_params,
        name="peer_v",
    )(idx_t, coef_t, h, _pack_table(peer_v))


def kernel(x, ln_mix_g, w_in, gmlp_norm_g, gmlp_w_s, gmlp_b_s, q_norm_g, k_norm_g, rel_bias, w_out, ln_ffn_g, peer_w_query, peer_sub_keys, peer_u, peer_v):
    bsz, seq, d = x.shape
    x2 = x.reshape(bsz * seq, d)
    for l in range(ln_mix_g.shape[0]):
        x2 = _layer(x2, bsz, seq, ln_mix_g[l], w_in[l], gmlp_norm_g[l], gmlp_w_s[l], gmlp_b_s[l],
                    q_norm_g[l], k_norm_g[l], rel_bias[l], w_out[l], ln_ffn_g[l],
                    peer_w_query[l], peer_sub_keys[l], peer_u[l], peer_v[l])
    return x2.reshape(bsz, seq, d)
```

```python
import functools

import jax
import jax.numpy as jnp
from jax import lax
from jax.experimental import pallas as pl
from jax.experimental.pallas import tpu as pltpu

F32 = jnp.float32
BF16 = jnp.bfloat16

D_MODEL = 1024
CHUNK = 64
MIX_A = 512
MIX_B = 512
A_GROUPS = 8
A_BLOCK = 128
B_HEADS = 8
B_HEAD_DIM = 64
B_LEFT = 8 * CHUNK
REL_CLIP = 128
PEER_HEADS = 8
PEER_NKEYS = 128
PEER_HALF = 128
PEER_TOPK = 16
PEER_PAIRS = PEER_HEADS * PEER_TOPK
EPS = 1e-6
NEG_INF = -1e30

LANES = 128
SUBLANES = 8
ROUTE_HEADS = 4
Q_TILE = 128
K_WIN = B_LEFT + Q_TILE
PROJ_ROWS = 512
TOK_TILE = 128
ROW_WORDS = D_MODEL // 2
ROW_SUB = ROW_WORDS // LANES
PSTRIDE = PEER_PAIRS + 8
VMEM_LIMIT = 48 * 1024 * 1024


def _rms(x, g):
    return x * lax.rsqrt(jnp.mean(x * x, axis=-1, keepdims=True) + EPS) * g


def _split_bf16(x):
    hi = x.astype(BF16)
    lo = (x - hi.astype(F32)).astype(BF16)
    return hi, lo


def _proj_kernel(x_ref, g_ref, w_ref, gn_ref, qg_ref, kg_ref, bd_ref,
                 ug_ref, vn_ref, q_ref, k_ref, v_ref):
    xb = _rms(x_ref[...], g_ref[...]).astype(BF16)

    def seg(j):
        return jnp.dot(xb, w_ref[:, j * MIX_A:(j + 1) * MIX_A], preferred_element_type=F32)

    def head_norm(z, gain):
        hi, lo = _split_bf16(z * z)
        bd = bd_ref[...]
        ss = (jnp.dot(hi, bd, preferred_element_type=F32)
              + jnp.dot(lo, bd, preferred_element_type=F32))
        return z * lax.rsqrt(ss * (1.0 / B_HEAD_DIM) + EPS) * gain

    ug_ref[...] = jax.nn.gelu(seg(0)).astype(BF16)
    vn_ref[...] = _rms(jax.nn.gelu(seg(1)), gn_ref[...]).astype(BF16)
    q_ref[...] = (head_norm(seg(2), qg_ref[...]) * (B_HEAD_DIM ** -0.5)).astype(BF16)
    k_ref[...] = head_norm(seg(3), kg_ref[...]).astype(BF16)
    v_ref[...] = seg(4).astype(BF16)


def _mixer_kernel(ug_ref, vn_ref, q_ref, kp_ref, vp_ref, ws_ref, bst_ref, bias_ref,
                  wo_ref, x_ref, gf_ref, h_ref, hn_ref):
    qstart = pl.multiple_of(pl.program_id(1) * Q_TILE, Q_TILE)

    row = lax.broadcasted_iota(jnp.int32, (A_BLOCK, A_BLOCK), 0)
    colp = lax.broadcasted_iota(jnp.int32, (A_BLOCK, A_BLOCK), 1)
    causal = (colp // CHUNK) <= (row // CHUNK)
    low = colp < CHUNK
    vn = vn_ref[...].astype(F32)
    sp_parts = []
    for p in range(A_GROUPS // 2):
        vp = vn[:, p * LANES:(p + 1) * LANES]
        v_lo = jnp.where(low, vp, 0.0).astype(BF16)
        v_hi = jnp.where(low, 0.0, vp).astype(BF16)
        w0 = jnp.where(causal, ws_ref[2 * p], 0.0).astype(BF16)
        w1 = jnp.where(causal, ws_ref[2 * p + 1], 0.0).astype(BF16)
        sp = (jnp.dot(w0, v_lo, preferred_element_type=F32)
              + jnp.dot(w1, v_hi, preferred_element_type=F32))
        b = jnp.where(low, bst_ref[:, 2 * p:2 * p + 1], bst_ref[:, 2 * p + 1:2 * p + 2])
        sp_parts.append(sp + b)
    a_out = ug_ref[...].astype(F32) * jnp.concatenate(sp_parts, axis=1)

    q = q_ref[...]
    kw = kp_ref[0, pl.ds(qstart, K_WIN), :]
    vw = vp_ref[0, pl.ds(qstart, K_WIN), :]
    kcol = lax.broadcasted_iota(jnp.int32, (Q_TILE, K_WIN), 1)
    in_seq = (kcol + qstart) >= B_LEFT
    outs = []
    for hh in range(B_HEADS):
        sl = slice(hh * B_HEAD_DIM, (hh + 1) * B_HEAD_DIM)
        s = lax.dot_general(q[:, sl], kw[:, sl], (((1,), (1,)), ((), ())),
                            preferred_element_type=F32)
        s = jnp.where(in_seq, s + bias_ref[hh], NEG_INF)
        e = jnp.exp(s - jnp.max(s, axis=-1, keepdims=True))
        p_att = (e / jnp.sum(e, axis=-1, keepdims=True)).astype(BF16)
        outs.append(jnp.dot(p_att, vw[:, sl], preferred_element_type=F32))
    b_out = jnp.concatenate(outs, axis=1)

    mix = jnp.concatenate([a_out.astype(BF16), b_out.astype(BF16)], axis=1)
    h = x_ref[...] + jnp.dot(mix, wo_ref[...], preferred_element_type=F32)
    h_ref[...] = h
    hn_ref[...] = _rms(h, gf_ref[...])


def _top16_rows(val, order, *tags):
    n = val.shape[0]
    outs = [[] for _ in range(2 + len(tags))]
    for _ in range(PEER_TOPK):
        nodes = [tuple(a[i:i + SUBLANES] for a in (val, order) + tags)
                 for i in range(0, n, SUBLANES)]
        while len(nodes) > 1:
            merged = []
            for a, b in zip(nodes[0::2], nodes[1::2]):
                take = a[0] >= b[0]
                merged.append(tuple(jnp.where(take, x, y) for x, y in zip(a, b)))
            if len(nodes) % 2:
                merged.append(nodes[-1])
            nodes = merged
        v8, o8 = nodes[0][0], nodes[0][1]
        m = jnp.max(v8, axis=0, keepdims=True)
        o = jnp.min(jnp.where(v8 == m, o8, jnp.inf), axis=0, keepdims=True)
        outs[0].append(m)
        outs[1].append(o)
        for j, t8 in enumerate(nodes[0][2:]):
            outs[2 + j].append(jnp.max(jnp.where(o8 == o, t8, -1.0), axis=0, keepdims=True))
        val = jnp.where(order == o, -jnp.inf, val)
    return [jnp.concatenate(o, axis=0) for o in outs]


def _route_kernel(hn_ref, wq_ref, keys_ref, idx_ref, gate_ref):
    hb = hn_ref[...].astype(BF16)
    q = jnp.dot(hb, wq_ref[...], preferred_element_type=F32)
    key_id = lax.broadcasted_iota(jnp.int32, (PEER_NKEYS, TOK_TILE), 0).astype(F32)
    r16 = lax.broadcasted_iota(jnp.int32, (PEER_TOPK, TOK_TILE), 0).astype(F32)
    r8 = r16[0:SUBLANES]
    for hh in range(ROUTE_HEADS):
        tops = []
        for p in range(2):
            col = (2 * hh + p) * PEER_HALF
            qp = q[:, col:col + PEER_HALF].astype(BF16)
            sc = lax.dot_general(keys_ref[hh, p], qp, (((1,), (1,)), ((), ())),
                                 preferred_element_type=F32)
            tops.append(_top16_rows(sc, key_id))
        (s1, i1), (s2, i2) = tops

        cand, flat, eid = [s1[0:1] + s2], [r16], [i1[0:1] * PEER_NKEYS + i2]
        for a in range(1, 8):
            cand.append(s1[a:a + 1] + s2[0:8])
            flat.append(r8 + a * PEER_TOPK)
            eid.append(i1[a:a + 1] * PEER_NKEYS + i2[0:8])
        cand.append(s1[8:16] + s2[0:1])
        flat.append((r8 + 8) * PEER_TOPK)
        eid.append(i1[8:16] * PEER_NKEYS + i2[0:1])
        best, _, chosen = _top16_rows(jnp.concatenate(cand, axis=0),
                                      jnp.concatenate(flat, axis=0),
                                      jnp.concatenate(eid, axis=0))
        ex = jnp.exp(best - best[0:1])
        rows = slice(hh * PEER_TOPK, (hh + 1) * PEER_TOPK)
        gate_ref[0, rows, :] = ex / jnp.sum(ex, axis=0, keepdims=True)
        idx_ref[0, rows, :] = chosen.astype(jnp.int32) * ROW_SUB


N_CHUNK = D_MODEL // LANES
GROUP = 8
N_GBUF = 2 * GROUP


def _gather_group(tbl_ref, idx_ref, gbufs, tok0):
    toks = [tok0 + i for i in range(GROUP)]
    for k in range(PEER_PAIRS):
        pair_idx = idx_ref.at[k]
        for i in range(GROUP):
            start = pl.multiple_of(pair_idx[toks[i]], ROW_SUB)
            gbufs[i][pl.ds(k, ROW_SUB, stride=PSTRIDE), :] = tbl_ref[pl.ds(start, ROW_SUB), :]


HALF_FIELDS = -0x70002000


def _plane(gbuf, j):
    w = pltpu.bitcast(gbuf[j * PSTRIDE:j * PSTRIDE + PEER_PAIRS], jnp.int32)
    lo = lax.shift_right_arithmetic(w << 16, 3) & HALF_FIELDS
    hi = lax.shift_right_arithmetic(w, 3) & HALF_FIELDS
    return pltpu.bitcast(lo, F32), pltpu.bitcast(hi, F32)


def _pipelined_tokens(gather, bulk, gbufs):
    set0, set1 = gbufs[:GROUP], gbufs[GROUP:]
    gather(set0, 0)

    def two_groups(j, carry):
        t0 = 2 * GROUP * j
        gather(set1, t0 + GROUP)
        for i in range(GROUP):
            bulk(set0[i], t0 + i)
        gather(set0, jnp.minimum(t0 + 2 * GROUP, TOK_TILE - GROUP))
        for i in range(GROUP):
            bulk(set1[i], t0 + GROUP + i)
        return carry

    lax.fori_loop(0, TOK_TILE // (2 * GROUP), two_groups, 0)


def _idx_copy(idx_hbm, tile, buf, sem):
    return pltpu.make_async_copy(idx_hbm.at[tile], buf, sem)


def _two_tiles(idx_hbm, idx_bufs, sems, process):
    step, n_steps = pl.program_id(0), pl.num_programs(0)

    @pl.when(step == 0)
    def _():
        _idx_copy(idx_hbm, 0, idx_bufs[0], sems.at[0]).start()

    _idx_copy(idx_hbm, 2 * step + 1, idx_bufs[1], sems.at[1]).start()
    _idx_copy(idx_hbm, 2 * step, idx_bufs[0], sems.at[0]).wait()
    process(0, idx_bufs[0])

    @pl.when(step + 1 < n_steps)
    def _():
        _idx_copy(idx_hbm, 2 * step + 2, idx_bufs[0], sems.at[0]).start()

    _idx_copy(idx_hbm, 2 * step + 1, idx_bufs[1], sems.at[1]).wait()
    process(1, idx_bufs[1])


def _peer_u_kernel(idx_hbm, mult_ref, x_ref, gate_ref, tbl_ref, c_ref, *scratch):
    gbufs, (hs, xbuf, idx_a, idx_b, sems) = scratch[:N_GBUF], scratch[N_GBUF:]
    lane = lax.broadcasted_iota(jnp.int32, (PEER_PAIRS, TOK_TILE), 1)
    _two_tiles(idx_hbm, (idx_a, idx_b), sems,
               functools.partial(_peer_u_tile, mult_ref, x_ref, gate_ref, tbl_ref, c_ref, gbufs,
                                 hs, xbuf, lane))


def _peer_u_tile(mult_ref, x_ref, gate_ref, tbl_ref, c_ref, gbufs, hs, xbuf, lane, tile, idx_ref):
    rows = slice(tile * TOK_TILE, (tile + 1) * TOK_TILE)
    hs[...] = jnp.zeros_like(hs)
    for j in range(N_CHUNK):
        xbuf[pl.ds(j, TOK_TILE, stride=N_CHUNK), :] = x_ref[rows, j * LANES:(j + 1) * LANES]

    def bulk(gbuf, tok):
        x8 = xbuf[pl.ds(pl.multiple_of(tok * N_CHUNK, N_CHUNK), N_CHUNK), :]
        acc = None
        for j in range(ROW_SUB):
            lo, hi = _plane(gbuf, j)
            term = lo * x8[j:j + 1] + hi * x8[ROW_SUB + j:ROW_SUB + j + 1]
            acc = term if acc is None else acc + term
        col = jnp.sum(acc, axis=1, keepdims=True)
        hs[...] = jnp.where(lane == tok, col, hs[...])

    _pipelined_tokens(functools.partial(_gather_group, tbl_ref, idx_ref), bulk, gbufs)
    c_ref[tile] = gate_ref[tile] * jax.nn.gelu(hs[...] * mult_ref[0])


def _peer_v_kernel(idx_hbm, mult_ref, ct_ref, h_ref, tbl_ref, out_ref, *scratch):
    gbufs, (c_hi, c_lo, obuf, idx_a, idx_b, sems) = scratch[:N_GBUF], scratch[N_GBUF:]
    tok_row = lax.broadcasted_iota(jnp.int32, (TOK_TILE, LANES), 0)
    _two_tiles(idx_hbm, (idx_a, idx_b), sems,
               functools.partial(_peer_v_tile, mult_ref, ct_ref, h_ref, tbl_ref, out_ref, gbufs,
                                 c_hi, c_lo, obuf, tok_row))


def _peer_v_tile(mult_ref, ct_ref, h_ref, tbl_ref, out_ref, gbufs, c_hi, c_lo, obuf, tok_row,
                 tile, idx_ref):
    rows = slice(tile * TOK_TILE, (tile + 1) * TOK_TILE)
    hi, lo = _split_bf16(ct_ref[tile])
    c_hi[...] = hi
    c_lo[...] = lo

    def bulk(gbuf, tok):
        sel = jnp.where(tok_row == tok, 1.0, 0.0).astype(BF16)
        cm = (jnp.dot(c_hi[...], sel, preferred_element_type=F32)
              + jnp.dot(c_lo[...], sel, preferred_element_type=F32))
        sums = [None] * N_CHUNK
        for j in range(ROW_SUB):
            lo, hi = _plane(gbuf, j)
            sums[j] = jnp.sum(lo * cm, axis=0, keepdims=True)
            sums[ROW_SUB + j] = jnp.sum(hi * cm, axis=0, keepdims=True)
        rows = pl.ds(pl.multiple_of(tok * N_CHUNK, N_CHUNK), N_CHUNK)
        obuf[rows, :] = jnp.concatenate(sums, axis=0)

    _pipelined_tokens(functools.partial(_gather_group, tbl_ref, idx_ref), bulk, gbufs)
    for j in range(N_CHUNK):
        cols = slice(j * LANES, (j + 1) * LANES)
        part = obuf[pl.ds(j, TOK_TILE, stride=N_CHUNK), :] * mult_ref[0]
        out_ref[rows, cols] = h_ref[rows, cols] + part


def _pack_table(tab):
    amax = jnp.maximum(jnp.max(jnp.abs(tab)), jnp.finfo(F32).tiny)
    k = 13 - jnp.clip(jnp.floor(jnp.log2(amax)), -100, 29).astype(jnp.int32)
    halves = (tab * jnp.ldexp(jnp.float32(1), k)).astype(jnp.float16)
    bits = lax.bitcast_convert_type(halves, jnp.uint16).astype(jnp.uint32)
    words = bits[:, :ROW_WORDS] | (bits[:, ROW_WORDS:] << 16)
    return words.reshape(-1, LANES), jnp.ldexp(jnp.float32(1), 112 - k).reshape(1)


def _attn_bias(rel_bias):
    r = jnp.arange(Q_TILE)[:, None]
    j = jnp.arange(K_WIN)[None, :]
    in_band = (j // CHUNK >= r // CHUNK) & (j // CHUNK <= r // CHUNK + B_LEFT // CHUNK)
    rb = rel_bias.astype(F32)
    period = K_WIN + Q_TILE - 1
    n_far = period - 2 * REL_CLIP + 1
    v = jnp.concatenate([jnp.broadcast_to(rb[:, 2 * REL_CLIP:], (B_HEADS, n_far)),
                         rb[:, 2 * REL_CLIP - 1:0:-1]], axis=1)
    v = jnp.roll(v, -(Q_TILE - 1), axis=1)
    skew = jnp.tile(v, (1, Q_TILE + 1))[:, :Q_TILE * (period - 1)]
    bias = skew.reshape(B_HEADS, Q_TILE, period - 1)[:, :, :K_WIN]
    return jnp.where(in_band[None], bias, NEG_INF)


def _resident(shape):
    return pl.BlockSpec(shape, lambda *_: (0,) * len(shape))


def _layer(x2, bsz, seq, ln_mix_g, w_in, gmlp_norm_g, gmlp_w_s, gmlp_b_s, q_norm_g, k_norm_g,
           rel_bias, w_out, ln_ffn_g, peer_w_query, peer_sub_keys, peer_u, peer_v):
    T = bsz * seq
    row2 = lambda a: a.reshape(1, -1).astype(F32)

    bd = jnp.kron(jnp.eye(B_HEADS, dtype=F32), jnp.ones((B_HEAD_DIM, B_HEAD_DIM), F32)).astype(BF16)
    col_spec = pl.BlockSpec((PROJ_ROWS, MIX_A), lambda i: (i, 0))
    ug, vn, qn, kn, vv = pl.pallas_call(
        _proj_kernel,
        grid=(T // PROJ_ROWS,),
        in_specs=[pl.BlockSpec((PROJ_ROWS, D_MODEL), lambda i: (i, 0)),
                  _resident((1, D_MODEL)),
                  _resident((D_MODEL, 2 * MIX_A + 3 * MIX_B)),
                  _resident((1, MIX_A)), _resident((1, MIX_B)), _resident((1, MIX_B)),
                  _resident((MIX_B, MIX_B))],
        out_specs=[col_spec] * 5,
        out_shape=[jax.ShapeDtypeStruct((T, MIX_A), BF16)] * 5,
        compiler_params=pltpu.CompilerParams(dimension_semantics=("parallel",),
                                             vmem_limit_bytes=VMEM_LIMIT),
        name="proj",
    )(x2, row2(ln_mix_g), w_in.astype(BF16), row2(gmlp_norm_g),
      row2(jnp.tile(q_norm_g, B_HEADS)), row2(jnp.tile(k_norm_g, B_HEADS)), bd)

    pad_left = lambda a: jnp.pad(a.reshape(bsz, seq, MIX_B), ((0, 0), (B_LEFT, 0), (0, 0)))
    n_qt = seq // Q_TILE
    tile_spec = lambda w: pl.BlockSpec((Q_TILE, w), lambda b, i: (b * n_qt + i, 0))
    seq_spec = pl.BlockSpec((1, seq + B_LEFT, MIX_B), lambda b, i: (b, 0, 0))
    h, hn = pl.pallas_call(
        _mixer_kernel,
        grid=(bsz, n_qt),
        in_specs=[tile_spec(MIX_A), tile_spec(MIX_A), tile_spec(MIX_B), seq_spec, seq_spec,
                  _resident((A_GROUPS, A_BLOCK, A_BLOCK)), _resident((A_BLOCK, A_GROUPS)),
                  _resident((B_HEADS, Q_TILE, K_WIN)), _resident((D_MODEL, D_MODEL)),
                  tile_spec(D_MODEL), _resident((1, D_MODEL))],
        out_specs=[tile_spec(D_MODEL), tile_spec(D_MODEL)],
        out_shape=[jax.ShapeDtypeStruct((T, D_MODEL), F32)] * 2,
        compiler_params=pltpu.CompilerParams(dimension_semantics=("parallel", "parallel"),
                                             vmem_limit_bytes=VMEM_LIMIT),
        name="mixer",
    )(ug, vn, qn, pad_left(kn), pad_left(vv), gmlp_w_s.astype(F32), gmlp_b_s.astype(F32).T,
      _attn_bias(rel_bias), w_out.astype(BF16), x2, row2(ln_ffn_g))

    n_tt = T // TOK_TILE
    qw = ROUTE_HEADS * 2 * PEER_HALF
    route_out = pl.BlockSpec((1, ROUTE_HEADS * PEER_TOPK, TOK_TILE), lambda i, hh: (i, hh, 0))
    idx_t, gate_t = pl.pallas_call(
        _route_kernel,
        grid=(n_tt, PEER_HEADS // ROUTE_HEADS),
        in_specs=[pl.BlockSpec((TOK_TILE, D_MODEL), lambda i, hh: (i, 0)),
                  pl.BlockSpec((D_MODEL, qw), lambda i, hh: (0, hh)),
                  pl.BlockSpec((ROUTE_HEADS, 2, PEER_NKEYS, PEER_HALF),
                               lambda i, hh: (hh, 0, 0, 0))],
        out_specs=[route_out, route_out],
        out_shape=[jax.ShapeDtypeStruct((n_tt, PEER_PAIRS, TOK_TILE), jnp.int32),
                   jax.ShapeDtypeStruct((n_tt, PEER_PAIRS, TOK_TILE), F32)],
        compiler_params=pltpu.CompilerParams(dimension_semantics=("parallel", "arbitrary"),
                                             vmem_limit_bytes=VMEM_LIMIT),
        name="route",
    )(hn, peer_w_query.astype(BF16), peer_sub_keys.astype(BF16))

    n_exp = peer_u.shape[0]
    tbl_spec = pl.BlockSpec((n_exp * ROW_SUB, LANES), lambda i: (0, 0),
                            pipeline_mode=pl.Buffered(1))
    idx_spec = pl.BlockSpec(memory_space=pl.ANY)
    mult_spec = pl.BlockSpec(memory_space=pltpu.SMEM)
    u_words, u_mult = _pack_table(peer_u)
    v_words, v_mult = _pack_table(peer_v)
    row_spec = pl.BlockSpec((2 * TOK_TILE, D_MODEL), lambda i: (i, 0))
    pair_spec = pl.BlockSpec((2, PEER_PAIRS, TOK_TILE), lambda i: (i, 0, 0))
    expert_params = pltpu.CompilerParams(dimension_semantics=("arbitrary",),
                                         vmem_limit_bytes=VMEM_LIMIT)
    gather_bufs = [pltpu.VMEM((ROW_SUB * PSTRIDE, LANES), jnp.uint32)] * N_GBUF
    pair_buf = lambda dt: pltpu.VMEM((PEER_PAIRS, TOK_TILE), dt)
    chunk_buf = pltpu.VMEM((TOK_TILE * N_CHUNK, LANES), F32)
    idx_bufs = [pltpu.SMEM((PEER_PAIRS, TOK_TILE), jnp.int32)] * 2 + [pltpu.SemaphoreType.DMA((2,))]
    coef_t = pl.pallas_call(
        _peer_u_kernel,
        grid=(n_tt // 2,),
        in_specs=[idx_spec, mult_spec, row_spec, pair_spec, tbl_spec],
        out_specs=pair_spec,
        out_shape=jax.ShapeDtypeStruct((n_tt, PEER_PAIRS, TOK_TILE), F32),
        scratch_shapes=gather_bufs + [pair_buf(F32), chunk_buf] + idx_bufs,
        compiler_params=expert_params,
        name="peer_u",
    )(idx_t, u_mult, hn, gate_t, u_words)
    return pl.pallas_call(
        _peer_v_kernel,
        grid=(n_tt // 2,),
        in_specs=[idx_spec, mult_spec, pair_spec, row_spec, tbl_spec],
        out_specs=row_spec,
        out_shape=jax.ShapeDtypeStruct((T, D_MODEL), F32),
        scratch_shapes=gather_bufs + [pair_buf(BF16), pair_buf(BF16), chunk_buf] + idx_bufs,
        compiler_params=expert_params,
        name="peer_v",
    )(idx_t, v_mult, coef_t, h, v_words)


def kernel(x, ln_mix_g, w_in, gmlp_norm_g, gmlp_w_s, gmlp_b_s, q_norm_g, k_norm_g, rel_bias, w_out, ln_ffn_g, peer_w_query, peer_sub_keys, peer_u, peer_v):
    bsz, seq, d = x.shape
    x2 = x.reshape(bsz * seq, d)
    for l in range(ln_mix_g.shape[0]):
        x2 = _layer(x2, bsz, seq, ln_mix_g[l], w_in[l], gmlp_norm_g[l], gmlp_w_s[l], gmlp_b_s[l],
                    q_norm_g[l], k_norm_g[l], rel_bias[l], w_out[l], ln_ffn_g[l],
                    peer_w_query[l], peer_sub_keys[l], peer_u[l], peer_v[l])
    return x2.reshape(bsz, seq, d)
```

```python
import functools

import jax
import jax.numpy as jnp
from jax import lax
from jax.experimental import pallas as pl
from jax.experimental.pallas import tpu as pltpu

F32 = jnp.float32
BF16 = jnp.bfloat16

D_MODEL = 1024
CHUNK = 64
MIX_A = 512
MIX_B = 512
A_GROUPS = 8
A_BLOCK = 128
B_HEADS = 8
B_HEAD_DIM = 64
B_LEFT = 8 * CHUNK
REL_CLIP = 128
PEER_HEADS = 8
PEER_NKEYS = 128
PEER_HALF = 128
PEER_TOPK = 16
PEER_PAIRS = PEER_HEADS * PEER_TOPK
EPS = 1e-6
NEG_INF = -1e30

LANES = 128
SUBLANES = 8
ROUTE_HEADS = 8
Q_TILE = 128
MIX_SUB = 2
ATT_ROWS = 128
K_WIN = B_LEFT + Q_TILE
PROJ_ROWS = 512
TOK_TILE = 128
ROW_WORDS = D_MODEL // 2
ROW_SUB = ROW_WORDS // LANES
PSTRIDE = PEER_PAIRS + 8
VMEM_LIMIT = 48 * 1024 * 1024


def _rms(x, g):
    return x * lax.rsqrt(jnp.mean(x * x, axis=-1, keepdims=True) + EPS) * g


def _split_bf16(x):
    hi = x.astype(BF16)
    lo = (x - hi.astype(F32)).astype(BF16)
    return hi, lo


def _proj_kernel(x_ref, g_ref, w_ref, gn_ref, qg_ref, kg_ref, bd_ref,
                 ug_ref, vn_ref, q_ref, k_ref, v_ref):
    xb = _rms(x_ref[...], g_ref[...]).astype(BF16)

    def seg(j):
        return jnp.dot(xb, w_ref[:, j * MIX_A:(j + 1) * MIX_A], preferred_element_type=F32)

    def head_norm(z, gain):
        hi, lo = _split_bf16(z * z)
        bd = bd_ref[...]
        ss = (jnp.dot(hi, bd, preferred_element_type=F32)
              + jnp.dot(lo, bd, preferred_element_type=F32))
        return z * lax.rsqrt(ss * (1.0 / B_HEAD_DIM) + EPS) * gain

    ug_ref[...] = jax.nn.gelu(seg(0)).astype(BF16)
    vn_ref[...] = _rms(jax.nn.gelu(seg(1)), gn_ref[...]).astype(BF16)
    q_ref[...] = (head_norm(seg(2), qg_ref[...]) * (B_HEAD_DIM ** -0.5)).astype(BF16)
    k_ref[...] = head_norm(seg(3), kg_ref[...]).astype(BF16)
    v_ref[...] = seg(4).astype(BF16)


def _mixer_kernel(ug_ref, vn_ref, q_ref, kp_ref, vp_ref, ws_ref, bst_ref, bias_ref,
                  wo_ref, x_ref, gf_ref, h_ref, hn_ref):
    for sub in range(MIX_SUB):
        _mixer_tile(ug_ref, vn_ref, q_ref, kp_ref, vp_ref, ws_ref, bst_ref, bias_ref, wo_ref,
                    x_ref, gf_ref, h_ref, hn_ref, sub)


def _mixer_tile(ug_ref, vn_ref, q_ref, kp_ref, vp_ref, ws_ref, bst_ref, bias_ref,
                wo_ref, x_ref, gf_ref, h_ref, hn_ref, sub):
    rows = slice(sub * Q_TILE, (sub + 1) * Q_TILE)
    qstart = pl.multiple_of((pl.program_id(1) * MIX_SUB + sub) * Q_TILE, Q_TILE)

    row = lax.broadcasted_iota(jnp.int32, (A_BLOCK, A_BLOCK), 0)
    colp = lax.broadcasted_iota(jnp.int32, (A_BLOCK, A_BLOCK), 1)
    causal = (colp // CHUNK) <= (row // CHUNK)
    low = colp < CHUNK
    vn = vn_ref[rows, :].astype(F32)
    sp_parts = []
    for p in range(A_GROUPS // 2):
        vp = vn[:, p * LANES:(p + 1) * LANES]
        v_lo = jnp.where(low, vp, 0.0).astype(BF16)
        v_hi = jnp.where(low, 0.0, vp).astype(BF16)
        w0 = jnp.where(causal, ws_ref[2 * p], 0.0).astype(BF16)
        w1 = jnp.where(causal, ws_ref[2 * p + 1], 0.0).astype(BF16)
        sp = (jnp.dot(w0, v_lo, preferred_element_type=F32)
              + jnp.dot(w1, v_hi, preferred_element_type=F32))
        b = jnp.where(low, bst_ref[:, 2 * p:2 * p + 1], bst_ref[:, 2 * p + 1:2 * p + 2])
        sp_parts.append(sp + b)
    a_out = ug_ref[rows, :].astype(F32) * jnp.concatenate(sp_parts, axis=1)

    q = q_ref[rows, :]
    kw = kp_ref[0, pl.ds(qstart, K_WIN), :]
    vw = vp_ref[0, pl.ds(qstart, K_WIN), :]
    kcol = lax.broadcasted_iota(jnp.int32, (ATT_ROWS, K_WIN), 1)
    in_seq = (kcol + qstart) >= B_LEFT
    outs = []
    for hh in range(B_HEADS):
        sl = slice(hh * B_HEAD_DIM, (hh + 1) * B_HEAD_DIM)
        blocks = []
        for r0 in range(0, Q_TILE, ATT_ROWS):
            rb = slice(r0, r0 + ATT_ROWS)
            s = lax.dot_general(q[rb, sl], kw[:, sl], (((1,), (1,)), ((), ())),
                                preferred_element_type=F32)
            s = jnp.where(in_seq, s + bias_ref[hh, rb, :], NEG_INF)
            e = jnp.exp(s - jnp.max(s, axis=-1, keepdims=True))
            p_att = (e / jnp.sum(e, axis=-1, keepdims=True)).astype(BF16)
            blocks.append(jnp.dot(p_att, vw[:, sl], preferred_element_type=F32))
        outs.append(jnp.concatenate(blocks, axis=0))
    b_out = jnp.concatenate(outs, axis=1)

    mix = jnp.concatenate([a_out.astype(BF16), b_out.astype(BF16)], axis=1)
    h = x_ref[rows, :] + jnp.dot(mix, wo_ref[...], preferred_element_type=F32)
    h_ref[rows, :] = h
    hn_ref[rows, :] = _rms(h, gf_ref[...])


def _top16_rows(val, order, *tags):
    n = val.shape[0]
    outs = [[] for _ in range(2 + len(tags))]
    for _ in range(PEER_TOPK):
        nodes = [tuple(a[i:i + SUBLANES] for a in (val, order) + tags)
                 for i in range(0, n, SUBLANES)]
        while len(nodes) > 1:
            merged = []
            for a, b in zip(nodes[0::2], nodes[1::2]):
                take = a[0] >= b[0]
                merged.append(tuple(jnp.where(take, x, y) for x, y in zip(a, b)))
            if len(nodes) % 2:
                merged.append(nodes[-1])
            nodes = merged
        v8, o8 = nodes[0][0], nodes[0][1]
        m = jnp.max(v8, axis=0, keepdims=True)
        o = jnp.min(jnp.where(v8 == m, o8, jnp.inf), axis=0, keepdims=True)
        outs[0].append(m)
        outs[1].append(o)
        for j, t8 in enumerate(nodes[0][2:]):
            outs[2 + j].append(jnp.max(jnp.where(o8 == o, t8, -1.0), axis=0, keepdims=True))
        val = jnp.where(order == o, -jnp.inf, val)
    return [jnp.concatenate(o, axis=0) for o in outs]


def _route_kernel(hn_ref, wq_ref, keys_ref, idx_ref, gate_ref):
    hb = hn_ref[...].astype(BF16)
    q = jnp.dot(hb, wq_ref[...], preferred_element_type=F32)
    key_id = lax.broadcasted_iota(jnp.int32, (PEER_NKEYS, TOK_TILE), 0).astype(F32)
    r16 = lax.broadcasted_iota(jnp.int32, (PEER_TOPK, TOK_TILE), 0).astype(F32)
    r8 = r16[0:SUBLANES]
    for hh in range(ROUTE_HEADS):
        tops = []
        for p in range(2):
            col = (2 * hh + p) * PEER_HALF
            qp = q[:, col:col + PEER_HALF].astype(BF16)
            sc = lax.dot_general(keys_ref[hh, p], qp, (((1,), (1,)), ((), ())),
                                 preferred_element_type=F32)
            tops.append(_top16_rows(sc, key_id))
        (s1, i1), (s2, i2) = tops

        cand, flat, eid = [s1[0:1] + s2], [r16], [i1[0:1] * PEER_NKEYS + i2]
        for a in range(1, 8):
            cand.append(s1[a:a + 1] + s2[0:8])
            flat.append(r8 + a * PEER_TOPK)
            eid.append(i1[a:a + 1] * PEER_NKEYS + i2[0:8])
        cand.append(s1[8:16] + s2[0:1])
        flat.append((r8 + 8) * PEER_TOPK)
        eid.append(i1[8:16] * PEER_NKEYS + i2[0:1])
        best, _, chosen = _top16_rows(jnp.concatenate(cand, axis=0),
                                      jnp.concatenate(flat, axis=0),
                                      jnp.concatenate(eid, axis=0))
        ex = jnp.exp(best - best[0:1])
        rows = slice(hh * PEER_TOPK, (hh + 1) * PEER_TOPK)
        gate_ref[0, rows, :] = ex / jnp.sum(ex, axis=0, keepdims=True)
        idx_ref[0, rows, :] = chosen.astype(jnp.int32) * ROW_SUB


N_CHUNK = D_MODEL // LANES
GROUP = 8
N_GBUF = 2 * GROUP


def _gather_token(tbl_ref, idx_ref, gbuf, tok):
    for k in range(PEER_PAIRS):
        start = pl.multiple_of(idx_ref.at[k][tok], ROW_SUB)
        gbuf[pl.ds(k, ROW_SUB, stride=PSTRIDE), :] = tbl_ref[pl.ds(start, ROW_SUB), :]


HALF_FIELDS = -0x70002000


def _plane(gbuf, j):
    w = pltpu.bitcast(gbuf[j * PSTRIDE:j * PSTRIDE + PEER_PAIRS], jnp.int32)
    lo = lax.shift_right_arithmetic(w << 16, 3) & HALF_FIELDS
    hi = lax.shift_right_arithmetic(w, 3) & HALF_FIELDS
    return pltpu.bitcast(lo, F32), pltpu.bitcast(hi, F32)


def _pipelined_tokens(gather, bulk, finish, gbufs):
    set0, set1 = gbufs[:GROUP], gbufs[GROUP:]
    for i in range(GROUP):
        gather(set0[i], i)

    def two_groups(j, carry, more=True):
        t0 = 2 * GROUP * j
        for i in range(GROUP):
            gather(set1[i], t0 + GROUP + i)
        finish([bulk(set0[i], t0 + i) for i in range(GROUP)], t0)
        if more:
            for i in range(GROUP):
                gather(set0[i], t0 + 2 * GROUP + i)
        finish([bulk(set1[i], t0 + GROUP + i) for i in range(GROUP)], t0 + GROUP)
        return carry

    n_trips = TOK_TILE // (2 * GROUP)
    lax.fori_loop(0, n_trips - 1, two_groups, 0)
    two_groups(n_trips - 1, 0, more=False)


def _idx_copy(idx_hbm, tile, buf, sem):
    return pltpu.make_async_copy(idx_hbm.at[tile], buf, sem)


def _two_tiles(idx_hbm, idx_bufs, sems, process):
    step, n_steps = pl.program_id(0), pl.num_programs(0)

    @pl.when(step == 0)
    def _():
        _idx_copy(idx_hbm, 0, idx_bufs[0], sems.at[0]).start()

    _idx_copy(idx_hbm, 2 * step + 1, idx_bufs[1], sems.at[1]).start()
    _idx_copy(idx_hbm, 2 * step, idx_bufs[0], sems.at[0]).wait()
    process(0, idx_bufs[0])

    @pl.when(step + 1 < n_steps)
    def _():
        _idx_copy(idx_hbm, 2 * step + 2, idx_bufs[0], sems.at[0]).start()

    _idx_copy(idx_hbm, 2 * step + 1, idx_bufs[1], sems.at[1]).wait()
    process(1, idx_bufs[1])


def _peer_u_kernel(idx_hbm, mult_ref, x_ref, gate_ref, tbl_ref, c_ref, *scratch):
    gbufs, (hs, xbuf, idx_a, idx_b, sems) = scratch[:N_GBUF], scratch[N_GBUF:]
    lane = lax.broadcasted_iota(jnp.int32, (PEER_PAIRS, TOK_TILE), 1)
    _two_tiles(idx_hbm, (idx_a, idx_b), sems,
               functools.partial(_peer_u_tile, mult_ref, x_ref, gate_ref, tbl_ref, c_ref, gbufs,
                                 hs, xbuf, lane))


def _peer_u_tile(mult_ref, x_ref, gate_ref, tbl_ref, c_ref, gbufs, hs, xbuf, lane, tile, idx_ref):
    rows = slice(tile * TOK_TILE, (tile + 1) * TOK_TILE)
    hs[...] = jnp.zeros_like(hs)
    for j in range(N_CHUNK):
        xbuf[pl.ds(j, TOK_TILE, stride=N_CHUNK), :] = x_ref[rows, j * LANES:(j + 1) * LANES]

    def bulk(gbuf, tok):
        x8 = xbuf[pl.ds(pl.multiple_of(tok * N_CHUNK, N_CHUNK), N_CHUNK), :]
        acc = None
        for j in range(ROW_SUB):
            lo, hi = _plane(gbuf, j)
            term = lo * x8[j:j + 1] + hi * x8[ROW_SUB + j:ROW_SUB + j + 1]
            acc = term if acc is None else acc + term
        return jnp.sum(acc, axis=1, keepdims=True)

    def finish(cols, tok0):
        cur = hs[...]
        for i, col in enumerate(cols):
            cur = jnp.where(lane == tok0 + i, col, cur)
        hs[...] = cur

    _pipelined_tokens(functools.partial(_gather_token, tbl_ref, idx_ref), bulk, finish, gbufs)
    c_ref[tile] = gate_ref[tile] * jax.nn.gelu(hs[...] * mult_ref[0])


def _peer_v_kernel(idx_hbm, mult_ref, ct_ref, h_ref, tbl_ref, out_ref, *scratch):
    gbufs, (c_hi, c_lo, obuf, idx_a, idx_b, sems) = scratch[:N_GBUF], scratch[N_GBUF:]
    tok_row = lax.broadcasted_iota(jnp.int32, (TOK_TILE, LANES), 0)
    _two_tiles(idx_hbm, (idx_a, idx_b), sems,
               functools.partial(_peer_v_tile, mult_ref, ct_ref, h_ref, tbl_ref, out_ref, gbufs,
                                 c_hi, c_lo, obuf, tok_row))


def _peer_v_tile(mult_ref, ct_ref, h_ref, tbl_ref, out_ref, gbufs, c_hi, c_lo, obuf, tok_row,
                 tile, idx_ref):
    rows = slice(tile * TOK_TILE, (tile + 1) * TOK_TILE)
    hi, lo = _split_bf16(ct_ref[tile])
    c_hi[...] = hi
    c_lo[...] = lo

    def bulk(gbuf, tok):
        sel = jnp.where(tok_row == tok, 1.0, 0.0).astype(BF16)
        cm = (jnp.dot(c_hi[...], sel, preferred_element_type=F32)
              + jnp.dot(c_lo[...], sel, preferred_element_type=F32))
        sums = [None] * N_CHUNK
        for j in range(ROW_SUB):
            lo, hi = _plane(gbuf, j)
            sums[j] = jnp.sum(lo * cm, axis=0, keepdims=True)
            sums[ROW_SUB + j] = jnp.sum(hi * cm, axis=0, keepdims=True)
        rows = pl.ds(pl.multiple_of(tok * N_CHUNK, N_CHUNK), N_CHUNK)
        obuf[rows, :] = jnp.concatenate(sums, axis=0)

    _pipelined_tokens(functools.partial(_gather_token, tbl_ref, idx_ref), bulk,
                      lambda results, tok0: None, gbufs)
    for j in range(N_CHUNK):
        cols = slice(j * LANES, (j + 1) * LANES)
        part = obuf[pl.ds(j, TOK_TILE, stride=N_CHUNK), :] * mult_ref[0]
        out_ref[rows, cols] = h_ref[rows, cols] + part


def _pack_table(tab):
    amax = jnp.maximum(jnp.max(jnp.abs(tab)), jnp.finfo(F32).tiny)
    k = 13 - jnp.clip(jnp.floor(jnp.log2(amax)), -100, 29).astype(jnp.int32)
    halves = (tab * jnp.ldexp(jnp.float32(1), k)).astype(jnp.float16)
    bits = lax.bitcast_convert_type(halves, jnp.uint16).astype(jnp.uint32)
    words = bits[:, :ROW_WORDS] | (bits[:, ROW_WORDS:] << 16)
    return words.reshape(-1, LANES), jnp.ldexp(jnp.float32(1), 112 - k).reshape(1)


def _attn_bias(rel_bias):
    r = jnp.arange(Q_TILE)[:, None]
    j = jnp.arange(K_WIN)[None, :]
    in_band = (j // CHUNK >= r // CHUNK) & (j // CHUNK <= r // CHUNK + B_LEFT // CHUNK)
    rb = rel_bias.astype(F32)
    period = K_WIN + Q_TILE - 1
    n_far = period - 2 * REL_CLIP + 1
    v = jnp.concatenate([jnp.broadcast_to(rb[:, 2 * REL_CLIP:], (B_HEADS, n_far)),
                         rb[:, 2 * REL_CLIP - 1:0:-1]], axis=1)
    v = jnp.roll(v, -(Q_TILE - 1), axis=1)
    skew = jnp.tile(v, (1, Q_TILE + 1))[:, :Q_TILE * (period - 1)]
    bias = skew.reshape(B_HEADS, Q_TILE, period - 1)[:, :, :K_WIN]
    return jnp.where(in_band[None], bias, NEG_INF)


def _resident(shape):
    return pl.BlockSpec(shape, lambda *_: (0,) * len(shape))


def _layer(x2, bsz, seq, ln_mix_g, w_in, gmlp_norm_g, gmlp_w_s, gmlp_b_s, q_norm_g, k_norm_g,
           rel_bias, w_out, ln_ffn_g, peer_w_query, peer_sub_keys, peer_u, peer_v):
    T = bsz * seq
    row2 = lambda a: a.reshape(1, -1).astype(F32)

    bd = jnp.kron(jnp.eye(B_HEADS, dtype=F32), jnp.ones((B_HEAD_DIM, B_HEAD_DIM), F32)).astype(BF16)
    col_spec = pl.BlockSpec((PROJ_ROWS, MIX_A), lambda i: (i, 0))
    ug, vn, qn, kn, vv = pl.pallas_call(
        _proj_kernel,
        grid=(T // PROJ_ROWS,),
        in_specs=[pl.BlockSpec((PROJ_ROWS, D_MODEL), lambda i: (i, 0)),
                  _resident((1, D_MODEL)),
                  _resident((D_MODEL, 2 * MIX_A + 3 * MIX_B)),
                  _resident((1, MIX_A)), _resident((1, MIX_B)), _resident((1, MIX_B)),
                  _resident((MIX_B, MIX_B))],
        out_specs=[col_spec] * 5,
        out_shape=[jax.ShapeDtypeStruct((T, MIX_A), BF16)] * 5,
        compiler_params=pltpu.CompilerParams(dimension_semantics=("parallel",),
                                             vmem_limit_bytes=VMEM_LIMIT),
        name="proj",
    )(x2, row2(ln_mix_g), w_in.astype(BF16), row2(gmlp_norm_g),
      row2(jnp.tile(q_norm_g, B_HEADS)), row2(jnp.tile(k_norm_g, B_HEADS)), bd)

    pad_left = lambda a: jnp.pad(a.reshape(bsz, seq, MIX_B), ((0, 0), (B_LEFT, 0), (0, 0)))
    n_qt = seq // (MIX_SUB * Q_TILE)
    tile_spec = lambda w: pl.BlockSpec((MIX_SUB * Q_TILE, w), lambda b, i: (b * n_qt + i, 0))
    seq_spec = pl.BlockSpec((1, seq + B_LEFT, MIX_B), lambda b, i: (b, 0, 0))
    h, hn = pl.pallas_call(
        _mixer_kernel,
        grid=(bsz, n_qt),
        in_specs=[tile_spec(MIX_A), tile_spec(MIX_A), tile_spec(MIX_B), seq_spec, seq_spec,
                  _resident((A_GROUPS, A_BLOCK, A_BLOCK)), _resident((A_BLOCK, A_GROUPS)),
                  _resident((B_HEADS, Q_TILE, K_WIN)), _resident((D_MODEL, D_MODEL)),
                  tile_spec(D_MODEL), _resident((1, D_MODEL))],
        out_specs=[tile_spec(D_MODEL), tile_spec(D_MODEL)],
        out_shape=[jax.ShapeDtypeStruct((T, D_MODEL), F32)] * 2,
        compiler_params=pltpu.CompilerParams(dimension_semantics=("parallel", "parallel"),
                                             vmem_limit_bytes=VMEM_LIMIT),
        name="mixer",
    )(ug, vn, qn, pad_left(kn), pad_left(vv), gmlp_w_s.astype(F32), gmlp_b_s.astype(F32).T,
      _attn_bias(rel_bias), w_out.astype(BF16), x2, row2(ln_ffn_g))

    n_tt = T // TOK_TILE
    qw = ROUTE_HEADS * 2 * PEER_HALF
    route_out = pl.BlockSpec((1, ROUTE_HEADS * PEER_TOPK, TOK_TILE), lambda i, hh: (i, hh, 0))
    idx_t, gate_t = pl.pallas_call(
        _route_kernel,
        grid=(n_tt, PEER_HEADS // ROUTE_HEADS),
        in_specs=[pl.BlockSpec((TOK_TILE, D_MODEL), lambda i, hh: (i, 0)),
                  pl.BlockSpec((D_MODEL, qw), lambda i, hh: (0, hh)),
                  pl.BlockSpec((ROUTE_HEADS, 2, PEER_NKEYS, PEER_HALF),
                               lambda i, hh: (hh, 0, 0, 0))],
        out_specs=[route_out, route_out],
        out_shape=[jax.ShapeDtypeStruct((n_tt, PEER_PAIRS, TOK_TILE), jnp.int32),
                   jax.ShapeDtypeStruct((n_tt, PEER_PAIRS, TOK_TILE), F32)],
        compiler_params=pltpu.CompilerParams(dimension_semantics=("parallel", "arbitrary"),
                                             vmem_limit_bytes=VMEM_LIMIT),
        name="route",
    )(hn, peer_w_query.astype(BF16), peer_sub_keys.astype(BF16))

    n_exp = peer_u.shape[0]
    tbl_spec = pl.BlockSpec((n_exp * ROW_SUB, LANES), lambda i: (0, 0),
                            pipeline_mode=pl.Buffered(1))
    idx_spec = pl.BlockSpec(memory_space=pl.ANY)
    mult_spec = pl.BlockSpec(memory_space=pltpu.SMEM)
    u_words, u_mult = _pack_table(peer_u)
    v_words, v_mult = _pack_table(peer_v)
    row_spec = pl.BlockSpec((2 * TOK_TILE, D_MODEL), lambda i: (i, 0))
    pair_spec = pl.BlockSpec((2, PEER_PAIRS, TOK_TILE), lambda i: (i, 0, 0))
    expert_params = pltpu.CompilerParams(dimension_semantics=("arbitrary",),
                                         vmem_limit_bytes=VMEM_LIMIT)
    gather_bufs = [pltpu.VMEM((ROW_SUB * PSTRIDE, LANES), jnp.uint32)] * N_GBUF
    pair_buf = lambda dt: pltpu.VMEM((PEER_PAIRS, TOK_TILE), dt)
    chunk_buf = pltpu.VMEM((TOK_TILE * N_CHUNK, LANES), F32)
    idx_bufs = [pltpu.SMEM((PEER_PAIRS, TOK_TILE), jnp.int32)] * 2 + [pltpu.SemaphoreType.DMA((2,))]
    coef_t = pl.pallas_call(
        _peer_u_kernel,
        grid=(n_tt // 2,),
        in_specs=[idx_spec, mult_spec, row_spec, pair_spec, tbl_spec],
        out_specs=pair_spec,
        out_shape=jax.ShapeDtypeStruct((n_tt, PEER_PAIRS, TOK_TILE), F32),
        scratch_shapes=gather_bufs + [pair_buf(F32), chunk_buf] + idx_bufs,
        compiler_params=expert_params,
        name="peer_u",
    )(idx_t, u_mult, hn, gate_t, u_words)
    return pl.pallas_call(
        _peer_v_kernel,
        grid=(n_tt // 2,),
        in_specs=[idx_spec, mult_spec, pair_spec, row_spec, tbl_spec],
        out_specs=row_spec,
        out_shape=jax.ShapeDtypeStruct((T, D_MODEL), F32),
        scratch_shapes=gather_bufs + [pair_buf(BF16), pair_buf(BF16), chunk_buf] + idx_bufs,
        compiler_params=expert_params,
        name="peer_v",
    )(idx_t, v_mult, coef_t, h, v_words)


def kernel(x, ln_mix_g, w_in, gmlp_norm_g, gmlp_w_s, gmlp_b_s, q_norm_g, k_norm_g, rel_bias, w_out, ln_ffn_g, peer_w_query, peer_sub_keys, peer_u, peer_v):
    bsz, seq, d = x.shape
    x2 = x.reshape(bsz * seq, d)
    for l in range(ln_mix_g.shape[0]):
        x2 = _layer(x2, bsz, seq, ln_mix_g[l], w_in[l], gmlp_norm_g[l], gmlp_w_s[l], gmlp_b_s[l],
                    q_norm_g[l], k_norm_g[l], rel_bias[l], w_out[l], ln_ffn_g[l],
                    peer_w_query[l], peer_sub_keys[l], peer_u[l], peer_v[l])
    return x2.reshape(bsz, seq, d)
```
